```python
import functools
import jax, jax.numpy as jnp
from jax import lax
import numpy as np

D_MODEL = 1024
BATCH = 8
SEQ = 4096
DEPTH = 4

GRID_W = 64
CTX_LEN = 256
N_MIXERS = 3
EPS = 1e-6

LRU_WIDTH = 1280
LRU_BLOCKS = 10
LRU_BLOCK = LRU_WIDTH // LRU_BLOCKS
LRU_CONV = 4
LRU_C = 8.0

RET_HEADS = 4
RET_DK = 256
RET_DV = 512
RET_DK_TOT = RET_HEADS * RET_DK
RET_DV_TOT = RET_HEADS * RET_DV
RET_CHUNK = 128
RET_THETA_BASE = 10000.0

SWA_HQ = 16
SWA_HKV = 4
SWA_DH = 64
SWA_WINDOW = 128
SWA_BLOCK = 128
ROPE_BASE = 10000.0

D_FF = 3584
N_EXPERTS = 8
TOP_K = 2

N_LAYERS_A = (DEPTH + 2) // 3
N_LAYERS_B = (DEPTH + 1) // 3
N_LAYERS_C = DEPTH // 3
N_DENSE = (DEPTH + 1) // 2
N_MOE = DEPTH // 2

kernel_name = 'hybrid_rglru_retention_swa_moe_dit'


def _rmsnorm(x, g):
    xf = x.astype(jnp.float32)
    y = xf * lax.rsqrt(jnp.mean(xf * xf, axis=-1, keepdims=True) + EPS)
    return (y * g.astype(jnp.float32)).astype(x.dtype)


def _modulate(x, g, shift, scale):
    return _rmsnorm(x, g) * (1 + scale) + shift


def _rotate(x, ang):
    cos = jnp.cos(ang)[:, None, :].astype(x.dtype)
    sin = jnp.sin(ang)[:, None, :].astype(x.dtype)
    x1, x2 = jnp.split(x, 2, axis=-1)
    return jnp.concatenate([x1 * cos - x2 * sin, x2 * cos + x1 * sin], axis=-1)


def _axial_angles(length, dim):
    rows = length // GRID_W
    row = jnp.repeat(jnp.arange(rows, dtype=jnp.float32), GRID_W)
    col = jnp.tile(jnp.arange(GRID_W, dtype=jnp.float32), rows)
    n_freq = dim // 4
    freq = ROPE_BASE ** (-jnp.arange(n_freq, dtype=jnp.float32) / n_freq)
    return jnp.concatenate([row[:, None] * freq, col[:, None] * freq], axis=-1)


def _ret_angles(length):
    n_freq = RET_DK // 2
    theta = RET_THETA_BASE ** (-jnp.arange(n_freq, dtype=jnp.float32) / n_freq)
    return jnp.arange(length, dtype=jnp.float32)[:, None] * theta


def _centred_dwconv(x, w, b):
    length = x.shape[1]
    left = LRU_CONV // 2
    right = LRU_CONV - 1 - left
    xp = jnp.pad(x, ((0, 0), (left, right), (0, 0)))
    return b + sum(xp[:, k:k + length] * w[k] for k in range(LRU_CONV))


def _rglru_coeffs(xr, w_gate, b_gate, lam):
    bsz, length, _ = xr.shape
    xb = xr.reshape(bsz, length, LRU_BLOCKS, LRU_BLOCK)
    gates = jnp.einsum('blnj,gnjk->gblnk', xb, w_gate).reshape(2, bsz, length, LRU_WIDTH)
    gates = gates + b_gate[:, None, None, :]
    r = jax.nn.sigmoid(gates[0])
    i = jax.nn.sigmoid(gates[1])
    log_a = -LRU_C * r * jax.nn.softplus(-lam)
    a = jnp.exp(log_a)
    bx = jnp.sqrt(-jnp.expm1(2.0 * log_a)) * (i * xr)
    return a, bx


def _linear_scan(a, bx, h0, reverse):
    def combine(e1, e2):
        a1, b1 = e1
        a2, b2 = e2
        return a1 * a2, a2 * b1 + b2
    a_cum, b_cum = lax.associative_scan(combine, (a, bx), axis=1, reverse=reverse)
    if h0 is None:
        return b_cum
    return a_cum * h0[:, None, :] + b_cum


def _mixer_rglru(hl, hc, w_in, conv_w, conv_b, w_gate, b_gate, lam, w_out, ctx_out):
    def branches(h):
        y, xr = jnp.split(h @ w_in, 2, axis=-1)
        xr = _centred_dwconv(xr, conv_w, conv_b).astype(jnp.float32)
        return jax.nn.gelu(y), xr
    yl, ul = branches(hl)
    yc, uc = branches(hc)
    h_lat, h_ctx_sum = None, None
    for d, rev in enumerate((False, True)):
        a_c, b_c = _rglru_coeffs(uc, w_gate[d], b_gate[d], lam[d])
        h_ctx = _linear_scan(a_c, b_c, None, rev)
        h0 = h_ctx[:, 0] if rev else h_ctx[:, -1]
        a_l, b_l = _rglru_coeffs(ul, w_gate[d], b_gate[d], lam[d])
        h_l = _linear_scan(a_l, b_l, h0, rev)
        h_lat = h_l if h_lat is None else h_lat + h_l
        h_ctx_sum = h_ctx if h_ctx_sum is None else h_ctx_sum + h_ctx
    ol = (yl * h_lat.astype(yl.dtype)) @ w_out
    oc = (yc * h_ctx_sum.astype(yc.dtype)) @ w_out if ctx_out else None
    return ol, oc


def _retention_dir(q, k, v, log_g, state0):
    bsz, length, heads, _ = q.shape
    dv = v.shape[-1]
    n_chunks = length // RET_CHUNK
    pos = jnp.arange(RET_CHUNK, dtype=jnp.float32)
    lg = log_g.astype(jnp.float32)[:, None]
    diff = pos[:, None] - pos[None, :]
    intra = jnp.where(diff >= 0, jnp.exp(lg[:, :, None] * jnp.maximum(diff, 0.0)), 0.0)
    q_dec = jnp.exp(lg * (pos + 1.0))
    k_dec = jnp.exp(lg * (RET_CHUNK - 1.0 - pos))
    c_dec = jnp.exp(lg * RET_CHUNK)

    def chunks(t):
        return t.reshape(bsz, n_chunks, RET_CHUNK, heads, -1).transpose(1, 0, 3, 2, 4)

    def step(state, qkv):
        qc, kc, vc = qkv
        s = jnp.einsum('bhid,bhjd->bhij', qc, kc) * intra
        o = jnp.einsum('bhij,bhje->bhie', s, vc)
        o = o + jnp.einsum('bhid,bhde->bhie', qc * q_dec[..., None], state)
        state = c_dec[..., None] * state + jnp.einsum('bhjd,bhje->bhde', kc * k_dec[..., None], vc)
        return state, o

    state, o = lax.scan(step, state0, (chunks(q), chunks(k), chunks(v)))
    return o.transpose(1, 0, 3, 2, 4).reshape(bsz, length, heads, dv), state


def _mixer_retention(hl, hc, w_in, log_decay, gn_gain, w_out, ctx_out):
    def project(h, rotate):
        bsz, length, _ = h.shape
        q, k, v, g = jnp.split(h @ w_in, [RET_DK_TOT, 2 * RET_DK_TOT, 2 * RET_DK_TOT + RET_DV_TOT], axis=-1)
        q = q.reshape(bsz, length, RET_HEADS, RET_DK)
        k = k.reshape(bsz, length, RET_HEADS, RET_DK) * (RET_DK ** -0.5)
        v = v.reshape(bsz, length, RET_HEADS, RET_DV)
        if rotate:
            ang = _ret_angles(length)
            q, k = _rotate(q, ang), _rotate(k, ang)
        return q.astype(jnp.float32), k.astype(jnp.float32), v.astype(jnp.float32), g

    def out(o, g):
        bsz, length = o.shape[:2]
        o = o * lax.rsqrt(jnp.mean(o * o, axis=-1, keepdims=True) + EPS)
        o = o.reshape(bsz, length, RET_DV_TOT).astype(g.dtype) * gn_gain
        return (jax.nn.silu(g) * o) @ w_out

    flip = lambda t: jnp.flip(t, axis=1)
    ql, kl, vl, gl = project(hl, True)
    qc, kc, vc, gc = project(hc, False)
    zero = jnp.zeros((hl.shape[0], RET_HEADS, RET_DK, RET_DV), jnp.float32)
    oc_f, st_f = _retention_dir(qc, kc, vc, log_decay[0], zero)
    oc_b, st_b = _retention_dir(flip(qc), flip(kc), flip(vc), log_decay[1], zero)
    ol_f, _ = _retention_dir(ql, kl, vl, log_decay[0], st_f)
    ol_b, _ = _retention_dir(flip(ql), flip(kl), flip(vl), log_decay[1], st_b)
    ol = out(ol_f + flip(ol_b), gl)
    oc = out(oc_f + flip(oc_b), gc) if ctx_out else None
    return ol, oc


def _mixer_swa(hl, hc, w_in, sink, w_out, ctx_out):
    def project(h):
        bsz, length, _ = h.shape
        q, k, v = jnp.split(h @ w_in, [SWA_HQ * SWA_DH, (SWA_HQ + SWA_HKV) * SWA_DH], axis=-1)
        return (q.reshape(bsz, length, SWA_HQ, SWA_DH),
                k.reshape(bsz, length, SWA_HKV, SWA_DH),
                v.reshape(bsz, length, SWA_HKV, SWA_DH))

    ql, kl, vl = project(hl)
    qc, kc, vc = project(hc)
    bsz, seq = hl.shape[:2]
    n_ctx = hc.shape[1]
    groups = SWA_HQ // SWA_HKV
    scale = SWA_DH ** -0.5
    sink_g = sink.reshape(SWA_HKV, groups).astype(jnp.float32)
    ang = _axial_angles(seq, SWA_DH)
    ql, kl = _rotate(ql, ang), _rotate(kl, ang)

    n_blk = seq // SWA_BLOCK
    qb = ql.reshape(bsz, n_blk, SWA_BLOCK, SWA_HKV, groups, SWA_DH)

    def band(t):
        tp = jnp.pad(t, ((0, 0), (SWA_BLOCK, SWA_BLOCK), (0, 0), (0, 0)))
        tp = tp.reshape(bsz, n_blk + 2, SWA_BLOCK, SWA_HKV, SWA_DH)
        return jnp.concatenate([tp[:, :-2], tp[:, 1:-1], tp[:, 2:]], axis=2)

    kb, vb = band(kl), band(vl)
    s_band = jnp.einsum('bnqhgd,bnkhd->bhgnqk', qb, kb, preferred_element_type=jnp.float32) * scale
    s_ctx = jnp.einsum('bnqhgd,bchd->bhgnqc', qb, kc, preferred_element_type=jnp.float32) * scale
    blk = jnp.arange(n_blk)[:, None, None]
    qpos = blk * SWA_BLOCK + jnp.arange(SWA_BLOCK)[None, :, None]
    kpos = (blk - 1) * SWA_BLOCK + jnp.arange(3 * SWA_BLOCK)[None, None, :]
    valid = (jnp.abs(qpos - kpos) <= SWA_WINDOW) & (kpos >= 0) & (kpos < seq)
    s_band = jnp.where(valid, s_band, -jnp.inf)
    sink_col = jnp.broadcast_to(sink_g[None, :, :, None, None, None],
                                (bsz, SWA_HKV, groups, n_blk, SWA_BLOCK, 1))
    p = jax.nn.softmax(jnp.concatenate([s_band, s_ctx, sink_col], axis=-1), axis=-1).astype(vl.dtype)
    pb = p[..., :3 * SWA_BLOCK]
    pc = p[..., 3 * SWA_BLOCK:3 * SWA_BLOCK + n_ctx]
    ol = jnp.einsum('bhgnqk,bnkhd->bnqhgd', pb, vb) + jnp.einsum('bhgnqc,bchd->bnqhgd', pc, vc)
    ol = ol.reshape(bsz, seq, SWA_HQ * SWA_DH) @ w_out
    oc = None
    if ctx_out:
        qcg = qc.reshape(bsz, n_ctx, SWA_HKV, groups, SWA_DH)
        sc = jnp.einsum('bqhgd,bkhd->bhgqk', qcg, kc, preferred_element_type=jnp.float32) * scale
        sink_c = jnp.broadcast_to(sink_g[None, :, :, None, None], (bsz, SWA_HKV, groups, n_ctx, 1))
        pcc = jax.nn.softmax(jnp.concatenate([sc, sink_c], axis=-1), axis=-1)[..., :n_ctx].astype(vc.dtype)
        oc = jnp.einsum('bhgqk,bkhd->bqhgd', pcc, vc).reshape(bsz, n_ctx, SWA_HQ * SWA_DH) @ w_out
    return ol, oc


def _swiglu(h, w_gu, w_down):
    g, u = jnp.split(h @ w_gu, 2, axis=-1)
    return (jax.nn.silu(g) * u) @ w_down


def _moe_swiglu(h, router, w_gu, w_down):
    logits = (h @ router).astype(jnp.float32)
    top_v, top_i = lax.top_k(logits, TOP_K)
    top_w = jax.nn.softmax(top_v, axis=-1)
    combine = jnp.sum(jax.nn.one_hot(top_i, N_EXPERTS, dtype=jnp.float32) * top_w[..., None], axis=1)
    combine = combine.astype(h.dtype)
    out = jnp.zeros_like(h)
    for e in range(N_EXPERTS):
        out = out + combine[:, e:e + 1] * _swiglu(h, w_gu[e], w_down[e])
    return out


def setup_inputs(seed: int = 0) -> dict:
    key = jax.random.key(seed)
    keys = iter(jax.random.split(key, 40))
    f32 = jnp.float32
    D = D_MODEL

    def nrm(shape, fan_in, scale=1.0):
        return scale * (fan_in ** -0.5) * jax.random.normal(next(keys), shape, f32)

    def noise(shape, scale):
        return scale * jax.random.normal(next(keys), shape, f32)

    x = jax.random.normal(next(keys), (BATCH, SEQ, D), f32)
    c = jax.random.normal(next(keys), (BATCH, D), f32)
    ctx = jax.random.normal(next(keys), (BATCH, CTX_LEN, D), f32)
    c_ctx = jax.random.normal(next(keys), (D,), f32)
    ada_w = nrm((DEPTH, D, 6 * D), D, 0.5)
    ada_b = noise((DEPTH, 6 * D), 0.02)
    norm_g = 1.0 + noise((DEPTH, 2, D), 0.05)
    final_g = 1.0 + noise((D,), 0.05)

    lru_w_in = nrm((N_LAYERS_A, D, 2 * LRU_WIDTH), D)
    lru_conv_w = nrm((N_LAYERS_A, LRU_CONV, LRU_WIDTH), LRU_CONV)
    lru_conv_b = noise((N_LAYERS_A, LRU_WIDTH), 0.02)
    lru_w_gate = nrm((N_LAYERS_A, 2, 2, LRU_BLOCKS, LRU_BLOCK, LRU_BLOCK), LRU_BLOCK)
    lru_b_gate = noise((N_LAYERS_A, 2, 2, LRU_WIDTH), 0.1)
    a_pow = jax.random.uniform(next(keys), (N_LAYERS_A, 2, LRU_WIDTH), f32, 0.9, 0.999)
    p = a_pow ** (1.0 / LRU_C)
    lru_lambda = jnp.log(p) - jnp.log1p(-p)
    lru_w_out = nrm((N_LAYERS_A, LRU_WIDTH, D), LRU_WIDTH)

    ret_w_in = nrm((N_LAYERS_B, D, 2 * RET_DK_TOT + 2 * RET_DV_TOT), D)
    head = jnp.arange(RET_HEADS, dtype=f32)
    jitter = jax.random.uniform(next(keys), (N_LAYERS_B, 2, RET_HEADS), f32, 0.0, 0.5)
    ret_log_decay = jnp.log(1.0 - 2.0 ** (-5.0 - head - jitter))
    ret_gn_gain = 1.0 + noise((N_LAYERS_B, RET_DV_TOT), 0.05)
    ret_w_out = nrm((N_LAYERS_B, RET_DV_TOT, D), RET_DV_TOT)

    swa_w_in = nrm((N_LAYERS_C, D, (SWA_HQ + 2 * SWA_HKV) * SWA_DH), D)
    swa_sink = noise((N_LAYERS_C, SWA_HQ), 0.5)
    swa_w_out = nrm((N_LAYERS_C, SWA_HQ * SWA_DH, D), SWA_HQ * SWA_DH)

    ffn_w_gu = nrm((N_DENSE, D, 2 * D_FF), D)
    ffn_w_down = nrm((N_DENSE, D_FF, D), D_FF)
    moe_router = nrm((N_MOE, D, N_EXPERTS), D)
    moe_w_gu = nrm((N_MOE, N_EXPERTS, D, 2 * D_FF), D)
    moe_w_down = nrm((N_MOE, N_EXPERTS, D_FF, D), D_FF)
    return {
        'x': x, 'c': c, 'ctx': ctx, 'c_ctx': c_ctx,
        'ada_w': ada_w, 'ada_b': ada_b, 'norm_g': norm_g, 'final_g': final_g,
        'lru_w_in': lru_w_in, 'lru_conv_w': lru_conv_w, 'lru_conv_b': lru_conv_b,
        'lru_w_gate': lru_w_gate, 'lru_b_gate': lru_b_gate, 'lru_lambda': lru_lambda,
        'lru_w_out': lru_w_out,
        'ret_w_in': ret_w_in, 'ret_log_decay': ret_log_decay, 'ret_gn_gain': ret_gn_gain,
        'ret_w_out': ret_w_out,
        'swa_w_in': swa_w_in, 'swa_sink': swa_sink, 'swa_w_out': swa_w_out,
        'ffn_w_gu': ffn_w_gu, 'ffn_w_down': ffn_w_down,
        'moe_router': moe_router, 'moe_w_gu': moe_w_gu, 'moe_w_down': moe_w_down,
    }


def reference(x, c, ctx, c_ctx, ada_w, ada_b, norm_g, final_g,
              lru_w_in, lru_conv_w, lru_conv_b, lru_w_gate, lru_b_gate, lru_lambda, lru_w_out,
              ret_w_in, ret_log_decay, ret_gn_gain, ret_w_out,
              swa_w_in, swa_sink, swa_w_out,
              ffn_w_gu, ffn_w_down,
              moe_router, moe_w_gu, moe_w_down):
    bsz, seq, dm = x.shape
    n_ctx = ctx.shape[1]
    cond_l = jax.nn.silu(c)
    cond_c = jax.nn.silu(c_ctx)
    xl, xc = x, ctx
    for i in range(DEPTH):
        last = i == DEPTH - 1
        mod_l = jnp.split((cond_l @ ada_w[i] + ada_b[i])[:, None, :], 6, axis=-1)
        mod_c = jnp.split(cond_c @ ada_w[i] + ada_b[i], 6, axis=-1)
        hl = _modulate(xl, norm_g[i, 0], mod_l[0], mod_l[1])
        hc = _modulate(xc, norm_g[i, 0], mod_c[0], mod_c[1])
        j = i // N_MIXERS
        if i % N_MIXERS == 0:
            ol, oc = _mixer_rglru(hl, hc, lru_w_in[j], lru_conv_w[j], lru_conv_b[j], lru_w_gate[j],
                                  lru_b_gate[j], lru_lambda[j], lru_w_out[j], not last)
        elif i % N_MIXERS == 1:
            ol, oc = _mixer_retention(hl, hc, ret_w_in[j], ret_log_decay[j], ret_gn_gain[j],
                                      ret_w_out[j], not last)
        else:
            ol, oc = _mixer_swa(hl, hc, swa_w_in[j], swa_sink[j], swa_w_out[j], not last)
        xl = xl + mod_l[2] * ol
        hl = _modulate(xl, norm_g[i, 1], mod_l[3], mod_l[4])
        f = i // 2
        if i % 2 == 0:
            ffn = functools.partial(_swiglu, w_gu=ffn_w_gu[f], w_down=ffn_w_down[f])
        else:
            ffn = functools.partial(_moe_swiglu, router=moe_router[f], w_gu=moe_w_gu[f], w_down=moe_w_down[f])
        if last:
            xl = xl + mod_l[5] * ffn(hl.reshape(-1, dm)).reshape(bsz, seq, dm)
        else:
            xc = xc + mod_c[2] * oc
            hc = _modulate(xc, norm_g[i, 1], mod_c[3], mod_c[4])
            y = ffn(jnp.concatenate([hl.reshape(-1, dm), hc.reshape(-1, dm)], axis=0))
            xl = xl + mod_l[5] * y[:bsz * seq].reshape(bsz, seq, dm)
            xc = xc + mod_c[5] * y[bsz * seq:].reshape(bsz, n_ctx, dm)
    return _rmsnorm(xl, final_g)
```

```python
import functools

import jax
import jax.numpy as jnp
from jax import lax
from jax.experimental import pallas as pl
from jax.experimental.pallas import tpu as pltpu

F32 = jnp.float32
BF16 = jnp.bfloat16
HIGHEST = lax.Precision.HIGHEST

EPS = 1e-6
N_MIXERS = 3
GRID_W = 64
LRU_C = 8.0
RET_DK = 256
RET_DV = 512
RET_CHUNK = 128
RET_THETA_BASE = 10000.0
SWA_DH = 64
SWA_BLOCK = 128
ROPE_BASE = 10000.0
TOP_K = 2

LANES = 128
SUBLANES = 8
TM = 512
LRU_TC = 256
FF_CHUNK = 512
MOE_TM = 512
VMEM_LIMIT = 56 * 1024 * 1024
NEG_BIG = -1e30


def _cparams(sem):
    return pltpu.CompilerParams(dimension_semantics=sem, vmem_limit_bytes=VMEM_LIMIT)


def _sigmoid(x):
    return 1.0 / (1.0 + jnp.exp(-x))


def _silu(x):
    return x * _sigmoid(x)


def _gelu_tanh(x):
    return 0.5 * x * (1.0 + jnp.tanh(0.7978845608028654 * (x + 0.044715 * (x * x * x))))


def _norm_mod(x, g, shift, scale):
    ms = jnp.mean(x * x, axis=-1, keepdims=True)
    y = (x * lax.rsqrt(ms + EPS)) * g
    return y * (1.0 + scale) + shift


def _dot(a, b):
    return jnp.dot(a, b, preferred_element_type=F32)


def _dot_nt(a, b):
    return lax.dot_general(a, b, (((1,), (1,)), ((), ())), preferred_element_type=F32)


def _dot_tn(a, b):
    return lax.dot_general(a, b, (((0,), (0,)), ((), ())), preferred_element_type=F32)


class _Layout:
    def __init__(self, bsz, seq, n_ctx):
        self.bsz, self.seq, self.n_ctx = bsz, seq, n_ctx
        self.n_lat = bsz * seq
        self.n_tok = self.n_lat + bsz * n_ctx
        assert seq % TM == 0 and (bsz * n_ctx) % TM == 0
        self.lat_tiles = self.n_lat // TM
        self.tiles_per_batch = seq // TM
        self.n_tiles = self.n_tok // TM

    def mod_row(self, i):
        return jnp.where(i < self.lat_tiles, i // self.tiles_per_batch, self.bsz)

    def pos_tile(self, i):
        return jnp.where(i < self.lat_tiles, i % self.tiles_per_batch, self.tiles_per_batch)

    def mod_spec(self, k, d):
        return pl.BlockSpec((1, 1, d), lambda i, *_: (self.mod_row(i), 0, k))


def _ada_kernel(cond_ref, w_ref, b_ref, o_ref):
    cnd = cond_ref[...]
    o_ref[0] = jnp.dot(_silu(cnd), w_ref[0], precision=HIGHEST, preferred_element_type=F32) + b_ref[0]


def _ada_mods(c, c_ctx, ada_w, ada_b):
    depth, d, d6 = ada_w.shape
    bsz = c.shape[0]
    rows = ((bsz + 1 + SUBLANES - 1) // SUBLANES) * SUBLANES
    cond = jnp.zeros((rows, d), F32).at[:bsz].set(c).at[bsz].set(c_ctx)
    tn = 1536
    assert d6 % tn == 0
    out = pl.pallas_call(
        _ada_kernel,
        grid=(depth, d6 // tn),
        in_specs=[pl.BlockSpec((rows, d), lambda l, n: (0, 0)),
                  pl.BlockSpec((1, d, tn), lambda l, n: (l, 0, n)),
                  pl.BlockSpec((1, 1, tn), lambda l, n: (l, 0, n))],
        out_specs=pl.BlockSpec((1, rows, tn), lambda l, n: (l, 0, n)),
        out_shape=jax.ShapeDtypeStruct((depth, rows, d6), F32),
        compiler_params=_cparams(("parallel", "parallel")),
        name="ada_mods",
    )(cond, ada_w, ada_b.reshape(depth, 1, d6))
    return out.reshape(depth, rows, 1, d6)


def _lru_in_kernel(x_ref, g_ref, sh_ref, sc_ref, w_ref, y_ref, xr_ref):
    width = y_ref.shape[1]
    h = _norm_mod(x_ref[...], g_ref[...], sh_ref[0], sc_ref[0]).astype(BF16)
    y_ref[...] = _dot(h, w_ref[:, :width]).astype(y_ref.dtype)
    xr_ref[...] = _dot(h, w_ref[:, width:]).astype(xr_ref.dtype)


def _lru_in(lay, x, g, mods, w_in):
    d = x.shape[1]
    width = w_in.shape[1] // 2
    return pl.pallas_call(
        _lru_in_kernel,
        grid=(lay.n_tiles,),
        in_specs=[pl.BlockSpec((TM, d), lambda i: (i, 0)),
                  pl.BlockSpec((1, d), lambda i: (0, 0)),
                  lay.mod_spec(0, d), lay.mod_spec(1, d),
                  pl.BlockSpec((d, 2 * width), lambda i: (0, 0))],
        out_specs=[pl.BlockSpec((TM, width), lambda i: (i, 0)),
                   pl.BlockSpec((TM, width), lambda i: (i, 0))],
        out_shape=[jax.ShapeDtypeStruct((lay.n_tok, width), BF16),
                   jax.ShapeDtypeStruct((lay.n_tok, width), F32)],
        compiler_params=_cparams(("parallel",)),
        name="lru_in",
    )(x, g.reshape(1, d), mods, mods, w_in.astype(BF16))


def _lru_chunk_order(lay, reverse):
    n_cc = lay.n_ctx // LRU_TC
    n_lc = lay.seq // LRU_TC

    def order(b, s):
        in_ctx = s < n_cc
        if reverse:
            cidx = jnp.where(in_ctx, n_cc - 1 - s, n_lc - 1 - (s - n_cc))
        else:
            cidx = jnp.where(in_ctx, s, s - n_cc)
        nseq = jnp.where(in_ctx, n_cc, n_lc)
        blk = jnp.where(in_ctx, lay.n_lat // LRU_TC + b * n_cc + cidx, b * n_lc + cidx)
        return cidx, nseq, blk

    return order, n_cc + n_lc


def _lru_scan_body(reverse, order, n_blocks, s, xp_ref, xc_ref, xn_ref, cw_ref, cb_ref, wg_ref, bg_ref,
                   lam_ref, ext_scr, a_scr, b_scr, h_scr, carry_scr):
    tc, width = xc_ref.shape
    bs = width // n_blocks
    cidx, nseq, _ = order(0, s)
    first = cidx == 0
    last = cidx == nseq - 1
    ext_scr[0:SUBLANES, :] = jnp.where(first, 0.0, xp_ref[...])
    ext_scr[SUBLANES:SUBLANES + tc, :] = xc_ref[...]
    ext_scr[SUBLANES + tc:2 * SUBLANES + tc, :] = jnp.where(last, 0.0, xn_ref[...])
    n_taps = cw_ref.shape[0]
    left = n_taps // 2
    conv = cb_ref[...] + sum(cw_ref[k:k + 1, :] * ext_scr[pl.ds(SUBLANES - left + k, tc), :]
                             for k in range(n_taps))
    xb = conv.astype(BF16)
    for n in range(n_blocks):
        cols = slice(n * bs, (n + 1) * bs)
        gates = _dot(xb[:, cols], wg_ref[n])
        r = _sigmoid(gates[:, :bs] + bg_ref[0:1, cols])
        i = _sigmoid(gates[:, bs:] + bg_ref[1:2, cols])
        nl = -lam_ref[:, cols]
        softplus = jnp.maximum(nl, 0.0) + jnp.log1p(jnp.exp(-jnp.abs(nl)))
        log_a = -LRU_C * r * softplus
        a = jnp.exp(log_a)
        a_scr[:, cols] = a
        b_scr[:, cols] = jnp.sqrt(1.0 - a * a) * (i * conv[:, cols])

    @pl.when(s == 0)
    def _():
        carry_scr[...] = jnp.zeros_like(carry_scr)

    def step(t, h):
        tt = tc - 1 - t if reverse else t
        h = a_scr[pl.ds(tt, 1), :] * h + b_scr[pl.ds(tt, 1), :]
        h_scr[pl.ds(tt, 1), :] = h
        return h

    carry_scr[...] = lax.fori_loop(0, tc, step, carry_scr[...], unroll=8)


def _lru_fwd_kernel(order, n_blocks, xp_ref, xc_ref, xn_ref, cw_ref, cb_ref, wg_ref, bg_ref, lam_ref,
                    hf_ref, ext_scr, a_scr, b_scr, h_scr, carry_scr):
    s = pl.program_id(1)
    _lru_scan_body(False, order, n_blocks, s, xp_ref, xc_ref, xn_ref, cw_ref, cb_ref, wg_ref, bg_ref,
                   lam_ref, ext_scr, a_scr, b_scr, h_scr, carry_scr)
    hf_ref[...] = h_scr[...].astype(hf_ref.dtype)


def _lru_bwd_kernel(order, n_blocks, xp_ref, xc_ref, xn_ref, cw_ref, cb_ref, wg_ref, bg_ref, lam_ref,
                    hf_ref, y_ref, x_ref, gate_ref, wo_ref, o_ref, ext_scr, a_scr, b_scr, h_scr, carry_scr):
    s = pl.program_id(1)
    _lru_scan_body(True, order, n_blocks, s, xp_ref, xc_ref, xn_ref, cw_ref, cb_ref, wg_ref, bg_ref,
                   lam_ref, ext_scr, a_scr, b_scr, h_scr, carry_scr)
    hsum = h_scr[...] + hf_ref[...].astype(F32)
    z = (_gelu_tanh(y_ref[...].astype(F32)) * hsum).astype(BF16)
    o_ref[...] = x_ref[...] + gate_ref[0] * _dot(z, wo_ref[...])


def _lru_mixer(lay, x, g, mods, w_in, conv_w, conv_b, w_gate, b_gate, lam, w_out):
    d = x.shape[1]
    y, xr = _lru_in(lay, x, g, mods, w_in)
    width = xr.shape[1]
    n_blocks, bs = w_gate.shape[2], w_gate.shape[3]
    tc = LRU_TC
    assert lay.seq % tc == 0 and lay.n_ctx % tc == 0
    halo_per_chunk = tc // SUBLANES
    n_halo = lay.n_tok // SUBLANES
    wg = jnp.concatenate([w_gate[:, 0], w_gate[:, 1]], axis=-1).astype(BF16)
    cb = conv_b.reshape(1, width)
    lam = lam.reshape(2, 1, width)
    scratch = [pltpu.VMEM((tc + 2 * SUBLANES, width), F32), pltpu.VMEM((tc, width), F32),
               pltpu.VMEM((tc, width), F32), pltpu.VMEM((tc, width), F32), pltpu.VMEM((1, width), F32)]

    def specs(order):
        blk = lambda b, s: order(b, s)[2]
        return [pl.BlockSpec((SUBLANES, width), lambda b, s: (jnp.maximum(blk(b, s) * halo_per_chunk - 1, 0), 0)),
                pl.BlockSpec((tc, width), lambda b, s: (blk(b, s), 0)),
                pl.BlockSpec((SUBLANES, width),
                             lambda b, s: (jnp.minimum((blk(b, s) + 1) * halo_per_chunk, n_halo - 1), 0)),
                pl.BlockSpec(conv_w.shape, lambda b, s: (0, 0)),
                pl.BlockSpec((1, width), lambda b, s: (0, 0))]

    def dir_specs(dr):
        return [pl.BlockSpec((None, n_blocks, bs, 2 * bs), lambda b, s: (dr, 0, 0, 0)),
                pl.BlockSpec((None, 2, width), lambda b, s: (dr, 0, 0)),
                pl.BlockSpec((None, 1, width), lambda b, s: (dr, 0, 0))]

    order_f, n_steps = _lru_chunk_order(lay, False)
    hf = pl.pallas_call(
        functools.partial(_lru_fwd_kernel, order_f, n_blocks),
        grid=(lay.bsz, n_steps),
        in_specs=specs(order_f) + dir_specs(0),
        out_specs=pl.BlockSpec((tc, width), lambda b, s: (order_f(b, s)[2], 0)),
        out_shape=jax.ShapeDtypeStruct((lay.n_tok, width), BF16),
        scratch_shapes=scratch,
        compiler_params=_cparams(("parallel", "arbitrary")),
        name="lru_scan_fwd",
    )(xr, xr, xr, conv_w, cb, wg, b_gate, lam)

    order_b, _ = _lru_chunk_order(lay, True)
    row = lambda b, s: (order_b(b, s)[2], 0)
    gate_row = lambda b, s: (jnp.where(s < lay.n_ctx // tc, lay.bsz, b), 0, 2)
    return pl.pallas_call(
        functools.partial(_lru_bwd_kernel, order_b, n_blocks),
        grid=(lay.bsz, n_steps),
        in_specs=specs(order_b) + dir_specs(1) + [
            pl.BlockSpec((tc, width), row), pl.BlockSpec((tc, width), row), pl.BlockSpec((tc, d), row),
            pl.BlockSpec((1, 1, d), gate_row), pl.BlockSpec((width, d), lambda b, s: (0, 0))],
        out_specs=pl.BlockSpec((tc, d), row),
        out_shape=jax.ShapeDtypeStruct((lay.n_tok, d), F32),
        scratch_shapes=scratch,
        compiler_params=_cparams(("parallel", "arbitrary")),
        name="lru_scan_bwd_out",
    )(xr, xr, xr, conv_w, cb, wg, b_gate, lam, hf, y, x, mods, w_out.astype(BF16))


def _ret_in_kernel(heads, x_ref, g_ref, sh_ref, sc_ref, w_ref, cos_ref, sin_ref, q_ref, k_ref, v_ref, gt_ref):
    h = _norm_mod(x_ref[...], g_ref[...], sh_ref[0], sc_ref[0]).astype(BF16)
    dk, half = RET_DK, RET_DK // 2
    dkt = heads * dk
    cos, sin = cos_ref[...], sin_ref[...]
    for which, out, scale in ((0, q_ref, 1.0), (1, k_ref, RET_DK ** -0.5)):
        for hd in range(heads):
            c0 = which * dkt + hd * dk
            r = _dot(h, w_ref[:, c0:c0 + dk]) * scale
            x1, x2 = r[:, :half], r[:, half:]
            out[:, hd * dk:hd * dk + half] = (x1 * cos - x2 * sin).astype(out.dtype)
            out[:, hd * dk + half:(hd + 1) * dk] = (x2 * cos + x1 * sin).astype(out.dtype)
    dvt = v_ref.shape[1]
    v_ref[...] = _dot(h, w_ref[:, 2 * dkt:2 * dkt + dvt]).astype(v_ref.dtype)
    gt_ref[...] = _dot(h, w_ref[:, 2 * dkt + dvt:]).astype(gt_ref.dtype)


def _ret_tables(lay):
    n_freq = RET_DK // 2
    theta = RET_THETA_BASE ** (-jnp.arange(n_freq, dtype=F32) / n_freq)
    ang = jnp.arange(lay.seq, dtype=F32)[:, None] * theta
    cos = jnp.concatenate([jnp.cos(ang), jnp.ones((TM, n_freq), F32)], axis=0)
    sin = jnp.concatenate([jnp.sin(ang), jnp.zeros((TM, n_freq), F32)], axis=0)
    return cos, sin


def _ret_in(lay, x, g, mods, w_in, heads):
    d = x.shape[1]
    dkt, dvt = heads * RET_DK, heads * RET_DV
    cos, sin = _ret_tables(lay)
    half = RET_DK // 2
    tok = lambda i: (i, 0)
    return pl.pallas_call(
        functools.partial(_ret_in_kernel, heads),
        grid=(lay.n_tiles,),
        in_specs=[pl.BlockSpec((TM, d), tok), pl.BlockSpec((1, d), lambda i: (0, 0)),
                  lay.mod_spec(0, d), lay.mod_spec(1, d),
                  pl.BlockSpec(w_in.shape, lambda i: (0, 0)),
                  pl.BlockSpec((TM, half), lambda i: (lay.pos_tile(i), 0)),
                  pl.BlockSpec((TM, half), lambda i: (lay.pos_tile(i), 0))],
        out_specs=[pl.BlockSpec((TM, dkt), tok), pl.BlockSpec((TM, dkt), tok),
                   pl.BlockSpec((TM, dvt), tok), pl.BlockSpec((TM, dvt), tok)],
        out_shape=[jax.ShapeDtypeStruct((lay.n_tok, dkt), BF16), jax.ShapeDtypeStruct((lay.n_tok, dkt), BF16),
                   jax.ShapeDtypeStruct((lay.n_tok, dvt), BF16), jax.ShapeDtypeStruct((lay.n_tok, dvt), BF16)],
        compiler_params=_cparams(("parallel",)),
        name="ret_in",
    )(x, g.reshape(1, d), mods, mods, w_in.astype(BF16), cos, sin)


def _ret_chunk_order(lay, reverse):
    n_cc = lay.n_ctx // RET_CHUNK
    n_lc = lay.seq // RET_CHUNK

    def blk(b, s):
        in_ctx = s < n_cc
        if reverse:
            cidx = jnp.where(in_ctx, n_cc - 1 - s, n_lc - 1 - (s - n_cc))
        else:
            cidx = jnp.where(in_ctx, s, s - n_cc)
        return jnp.where(in_ctx, lay.n_lat // RET_CHUNK + b * n_cc + cidx, b * n_lc + cidx)

    return blk, n_cc + n_lc


def _ret_chunk(reverse, heads, ld_ref, q_ref, k_ref, v_ref, state_scr, emit):
    c = RET_CHUNK
    dk, dv = RET_DK, RET_DV
    dr = 1 if reverse else 0

    @pl.when(pl.program_id(1) == 0)
    def _():
        state_scr[...] = jnp.zeros_like(state_scr)

    ri = lax.broadcasted_iota(jnp.int32, (c, c), 0)
    ci = lax.broadcasted_iota(jnp.int32, (c, c), 1)
    diff = (ci - ri if reverse else ri - ci).astype(F32)
    pos = lax.broadcasted_iota(jnp.int32, (c, dk), 0).astype(F32)
    for h in range(heads):
        lg = ld_ref[dr, h]
        intra = jnp.where(diff >= 0, jnp.exp(lg * jnp.maximum(diff, 0.0)), 0.0)
        if reverse:
            q_dec, k_dec = jnp.exp(lg * (c - pos)), jnp.exp(lg * pos)
        else:
            q_dec, k_dec = jnp.exp(lg * (pos + 1.0)), jnp.exp(lg * (c - 1.0 - pos))
        c_dec = jnp.exp(jnp.full((1, dv), lg, F32) * float(c))
        qh = q_ref[:, h * dk:(h + 1) * dk]
        kh = k_ref[:, h * dk:(h + 1) * dk]
        vh = v_ref[:, h * dv:(h + 1) * dv]
        sc = _dot_nt(qh, kh) * intra
        st = state_scr[h]
        o = _dot(sc.astype(BF16), vh) + _dot((qh.astype(F32) * q_dec).astype(BF16), st.astype(BF16))
        kd = (kh.astype(F32) * k_dec).astype(BF16)
        state_scr[h] = c_dec * st + _dot_tn(kd, vh)
        emit(h, o)


def _ret_fwd_kernel(heads, ld_ref, q_ref, k_ref, v_ref, of_ref, state_scr):
    def emit(h, o):
        of_ref[:, h * RET_DV:(h + 1) * RET_DV] = o.astype(of_ref.dtype)

    _ret_chunk(False, heads, ld_ref, q_ref, k_ref, v_ref, state_scr, emit)


def _ret_bwd_kernel(heads, ld_ref, q_ref, k_ref, v_ref, of_ref, gt_ref, gain_ref, z_ref, state_scr):
    def emit(h, o):
        cols = slice(h * RET_DV, (h + 1) * RET_DV)
        osum = o + of_ref[:, cols].astype(F32)
        on = osum * lax.rsqrt(jnp.mean(osum * osum, axis=-1, keepdims=True) + EPS)
        z_ref[:, cols] = (_silu(gt_ref[:, cols].astype(F32)) * (on * gain_ref[:, cols])).astype(z_ref.dtype)

    _ret_chunk(True, heads, ld_ref, q_ref, k_ref, v_ref, state_scr, emit)


def _ret_mixer(lay, x, g, mods, w_in, log_decay, gn_gain, w_out):
    heads = log_decay.shape[1]
    dkt, dvt = heads * RET_DK, heads * RET_DV
    q, k, v, gt = _ret_in(lay, x, g, mods, w_in, heads)
    c = RET_CHUNK
    assert lay.seq % c == 0 and lay.n_ctx % c == 0
    state = [pltpu.VMEM((heads, RET_DK, RET_DV), F32)]
    smem = pl.BlockSpec(memory_space=pltpu.SMEM)

    blk_f, n_steps = _ret_chunk_order(lay, False)
    row_f = lambda b, s: (blk_f(b, s), 0)
    of = pl.pallas_call(
        functools.partial(_ret_fwd_kernel, heads),
        grid=(lay.bsz, n_steps),
        in_specs=[smem, pl.BlockSpec((c, dkt), row_f), pl.BlockSpec((c, dkt), row_f), pl.BlockSpec((c, dvt), row_f)],
        out_specs=pl.BlockSpec((c, dvt), row_f),
        out_shape=jax.ShapeDtypeStruct((lay.n_tok, dvt), BF16),
        scratch_shapes=state,
        compiler_params=_cparams(("parallel", "arbitrary")),
        name="ret_fwd",
    )(log_decay, q, k, v)

    blk_b, _ = _ret_chunk_order(lay, True)
    row_b = lambda b, s: (blk_b(b, s), 0)
    z = pl.pallas_call(
        functools.partial(_ret_bwd_kernel, heads),
        grid=(lay.bsz, n_steps),
        in_specs=[smem, pl.BlockSpec((c, dkt), row_b), pl.BlockSpec((c, dkt), row_b), pl.BlockSpec((c, dvt), row_b),
                  pl.BlockSpec((c, dvt), row_b), pl.BlockSpec((c, dvt), row_b),
                  pl.BlockSpec((1, dvt), lambda b, s: (0, 0))],
        out_specs=pl.BlockSpec((c, dvt), row_b),
        out_shape=jax.ShapeDtypeStruct((lay.n_tok, dvt), BF16),
        scratch_shapes=state,
        compiler_params=_cparams(("parallel", "arbitrary")),
        name="ret_bwd_norm",
    )(log_decay, q, k, v, of, gt, gn_gain.reshape(1, dvt))
    return _out_proj(lay, z, w_out, x, mods, 2)


def _swa_in_kernel(n_rot, x_ref, g_ref, sh_ref, sc_ref, w_ref, cos_ref, sin_ref, q_ref, k_ref, v_ref):
    h = _norm_mod(x_ref[...], g_ref[...], sh_ref[0], sc_ref[0]).astype(BF16)
    cos, sin = cos_ref[...], sin_ref[...]
    lane = lax.broadcasted_iota(jnp.int32, cos.shape, 1)
    low = (lane % SWA_DH) < SWA_DH // 2
    nq = q_ref.shape[1]
    for c0 in range(0, n_rot, LANES):
        r = _dot(h, w_ref[:, c0:c0 + LANES])
        partner = jnp.where(low, pltpu.roll(r, LANES - SWA_DH // 2, 1), pltpu.roll(r, SWA_DH // 2, 1))
        rot = r * cos + partner * sin
        if c0 < nq:
            q_ref[:, c0:c0 + LANES] = (rot * (SWA_DH ** -0.5)).astype(q_ref.dtype)
        else:
            k_ref[:, c0 - nq:c0 - nq + LANES] = rot.astype(k_ref.dtype)
    v_ref[...] = _dot(h, w_ref[:, n_rot:]).astype(v_ref.dtype)


def _swa_tables(lay):
    length = lay.seq
    rows = length // GRID_W
    row = jnp.repeat(jnp.arange(rows, dtype=F32), GRID_W)
    col = jnp.tile(jnp.arange(GRID_W, dtype=F32), rows)
    n_freq = SWA_DH // 4
    freq = ROPE_BASE ** (-jnp.arange(n_freq, dtype=F32) / n_freq)
    ang = jnp.concatenate([row[:, None] * freq, col[:, None] * freq], axis=-1)
    reps = LANES // (SWA_DH // 2)
    cos = jnp.tile(jnp.cos(ang), (1, reps))
    sign = jnp.tile(jnp.concatenate([-jnp.ones((SWA_DH // 2,), F32), jnp.ones((SWA_DH // 2,), F32)]), LANES // SWA_DH)
    sin = jnp.tile(jnp.sin(ang), (1, reps)) * sign
    cos = jnp.concatenate([cos, jnp.ones((TM, LANES), F32)], axis=0)
    sin = jnp.concatenate([sin, jnp.zeros((TM, LANES), F32)], axis=0)
    return cos, sin


def _swa_in(lay, x, g, mods, w_in, hq, hkv):
    d = x.shape[1]
    nq, nkv = hq * SWA_DH, hkv * SWA_DH
    cos, sin = _swa_tables(lay)
    tok = lambda i: (i, 0)
    return pl.pallas_call(
        functools.partial(_swa_in_kernel, nq + nkv),
        grid=(lay.n_tiles,),
        in_specs=[pl.BlockSpec((TM, d), tok), pl.BlockSpec((1, d), lambda i: (0, 0)),
                  lay.mod_spec(0, d), lay.mod_spec(1, d),
                  pl.BlockSpec(w_in.shape, lambda i: (0, 0)),
                  pl.BlockSpec((TM, LANES), lambda i: (lay.pos_tile(i), 0)),
                  pl.BlockSpec((TM, LANES), lambda i: (lay.pos_tile(i), 0))],
        out_specs=[pl.BlockSpec((TM, nq), tok), pl.BlockSpec((TM, nkv), tok), pl.BlockSpec((TM, nkv), tok)],
        out_shape=[jax.ShapeDtypeStruct((lay.n_tok, nq), BF16), jax.ShapeDtypeStruct((lay.n_tok, nkv), BF16),
                   jax.ShapeDtypeStruct((lay.n_tok, nkv), BF16)],
        compiler_params=_cparams(("parallel",)),
        name="swa_in",
    )(x, g.reshape(1, d), mods, mods, w_in.astype(BF16), cos, sin)


def _softmax_pv(s, sink, v):
    m = jnp.maximum(jnp.max(s, axis=-1, keepdims=True), sink)
    p = jnp.exp(s - m)
    denom = jnp.sum(p, axis=-1, keepdims=True) + jnp.exp(sink - m)
    return _dot(p.astype(BF16), v) / denom


def _swa_attn_kernel(n_qb, hq, hkv, sink_ref, q_ref, kp_ref, kc_ref, kn_ref, kx_ref, vp_ref, vc_ref, vn_ref,
                     vx_ref, o_ref, k_scr, v_scr):
    j = pl.program_id(1)
    blk = SWA_BLOCK
    n_ctx = kx_ref.shape[0]
    groups = hq // hkv
    dh = SWA_DH

    @pl.when(j < n_qb)
    def _():
        k_scr[0:blk, :] = kp_ref[...]
        k_scr[blk:2 * blk, :] = kc_ref[...]
        k_scr[2 * blk:3 * blk, :] = kn_ref[...]
        k_scr[3 * blk:, :] = kx_ref[...]
        v_scr[0:blk, :] = vp_ref[...]
        v_scr[blk:2 * blk, :] = vc_ref[...]
        v_scr[2 * blk:3 * blk, :] = vn_ref[...]
        v_scr[3 * blk:, :] = vx_ref[...]
        nk = 3 * blk + n_ctx
        r = lax.broadcasted_iota(jnp.int32, (blk, nk), 0)
        c = lax.broadcasted_iota(jnp.int32, (blk, nk), 1)
        lo = jnp.maximum(r, jnp.where(j > 0, 0, blk))
        hi = jnp.minimum(r + 2 * blk, jnp.where(j < n_qb - 1, 3 * blk, 2 * blk) - 1)
        bias = jnp.where(c >= 3 * blk, 0.0, jnp.where(c < lo, NEG_BIG, jnp.where(c > hi, NEG_BIG, 0.0)))
        for hk in range(hkv):
            kh = k_scr[:, hk * dh:(hk + 1) * dh]
            vh = v_scr[:, hk * dh:(hk + 1) * dh]
            for gq in range(groups):
                h = hk * groups + gq
                s = _dot_nt(q_ref[:, h * dh:(h + 1) * dh], kh) + bias
                o_ref[:, h * dh:(h + 1) * dh] = _softmax_pv(s, sink_ref[h], vh).astype(o_ref.dtype)

    @pl.when(j >= n_qb)
    def _():
        for hk in range(hkv):
            kh = kx_ref[:, hk * dh:(hk + 1) * dh]
            vh = vx_ref[:, hk * dh:(hk + 1) * dh]
            for gq in range(groups):
                h = hk * groups + gq
                s = _dot_nt(q_ref[:, h * dh:(h + 1) * dh], kh)
                o_ref[:, h * dh:(h + 1) * dh] = _softmax_pv(s, sink_ref[h], vh).astype(o_ref.dtype)


def _swa_mixer(lay, x, g, mods, w_in, sink, w_out):
    hq = sink.shape[0]
    hkv = (w_in.shape[1] // SWA_DH - hq) // 2
    nq, nkv = hq * SWA_DH, hkv * SWA_DH
    q, k, v = _swa_in(lay, x, g, mods, w_in, hq, hkv)
    blk = SWA_BLOCK
    n_qb, n_cb = lay.seq // blk, lay.n_ctx // blk
    assert lay.seq % blk == 0 and lay.n_ctx % blk == 0
    ctx_blk0 = lay.n_lat // blk

    def qrow(b, j):
        return (jnp.where(j < n_qb, b * n_qb + j, ctx_blk0 + b * n_cb + (j - n_qb)), 0)

    def band(off):
        return lambda b, j: (b * n_qb + jnp.clip(jnp.minimum(j, n_qb - 1) + off, 0, n_qb - 1), 0)

    ctx_row = lambda b, j: (lay.n_lat // lay.n_ctx + b, 0)
    kv_specs = [pl.BlockSpec((blk, nkv), band(-1)), pl.BlockSpec((blk, nkv), band(0)),
                pl.BlockSpec((blk, nkv), band(1)), pl.BlockSpec((lay.n_ctx, nkv), ctx_row)]
    o = pl.pallas_call(
        functools.partial(_swa_attn_kernel, n_qb, hq, hkv),
        grid=(lay.bsz, n_qb + n_cb),
        in_specs=[pl.BlockSpec(memory_space=pltpu.SMEM), pl.BlockSpec((blk, nq), qrow)] + kv_specs + kv_specs,
        out_specs=pl.BlockSpec((blk, nq), qrow),
        out_shape=jax.ShapeDtypeStruct((lay.n_tok, nq), BF16),
        scratch_shapes=[pltpu.VMEM((3 * blk + lay.n_ctx, nkv), BF16), pltpu.VMEM((3 * blk + lay.n_ctx, nkv), BF16)],
        compiler_params=_cparams(("parallel", "arbitrary")),
        name="swa_attn",
    )(sink, q, k, k, k, k, v, v, v, v)
    return _out_proj(lay, o, w_out, x, mods, 2)


def _out_proj_kernel(a_ref, w_ref, x_ref, gate_ref, o_ref):
    o_ref[...] = x_ref[...] + gate_ref[0] * _dot(a_ref[...], w_ref[...])


def _out_proj(lay, a, w, x, mods, gate_idx):
    d = x.shape[1]
    kdim = a.shape[1]
    tok = lambda i: (i, 0)
    return pl.pallas_call(
        _out_proj_kernel,
        grid=(lay.n_tiles,),
        in_specs=[pl.BlockSpec((TM, kdim), tok), pl.BlockSpec((kdim, d), lambda i: (0, 0)),
                  pl.BlockSpec((TM, d), tok), lay.mod_spec(gate_idx, d)],
        out_specs=pl.BlockSpec((TM, d), tok),
        out_shape=jax.ShapeDtypeStruct((lay.n_tok, d), F32),
        compiler_params=_cparams(("parallel",)),
        name="out_proj",
    )(a, w.astype(BF16), x, mods)


def _swiglu_step(h, wg_ref, wu_ref, wd_ref, acc_ref):
    gg = _dot(h, wg_ref[...])
    uu = _dot(h, wu_ref[...])
    acc_ref[...] += _dot((_silu(gg) * uu).astype(BF16), wd_ref[...])


def _ffn_kernel(x_ref, g_ref, sh_ref, sc_ref, gate_ref, wg_ref, wu_ref, wd_ref, o_ref, h_scr, acc_scr):
    c = pl.program_id(1)

    @pl.when(c == 0)
    def _():
        h_scr[...] = _norm_mod(x_ref[...], g_ref[...], sh_ref[0], sc_ref[0]).astype(BF16)
        acc_scr[...] = jnp.zeros_like(acc_scr)

    _swiglu_step(h_scr[...], wg_ref, wu_ref, wd_ref, acc_scr)

    @pl.when(c == pl.num_programs(1) - 1)
    def _():
        o_ref[...] = x_ref[...] + gate_ref[0] * acc_scr[...]


def _dense_ffn(lay, x, g, mods, w_gu, w_down, n_tiles):
    d = x.shape[1]
    d_ff = w_down.shape[0]
    assert d_ff % FF_CHUNK == 0
    n_c = d_ff // FF_CHUNK
    tok = lambda i, c: (i, 0)
    w_gu = w_gu.astype(BF16)
    return pl.pallas_call(
        _ffn_kernel,
        grid=(n_tiles, n_c),
        in_specs=[pl.BlockSpec((TM, d), tok), pl.BlockSpec((1, d), lambda i, c: (0, 0)),
                  lay.mod_spec(3, d), lay.mod_spec(4, d), lay.mod_spec(5, d),
                  pl.BlockSpec((d, FF_CHUNK), lambda i, c: (0, c)),
                  pl.BlockSpec((d, FF_CHUNK), lambda i, c: (0, n_c + c)),
                  pl.BlockSpec((FF_CHUNK, d), lambda i, c: (c, 0))],
        out_specs=pl.BlockSpec((TM, d), tok),
        out_shape=jax.ShapeDtypeStruct((n_tiles * TM, d), F32),
        scratch_shapes=[pltpu.VMEM((TM, d), BF16), pltpu.VMEM((TM, d), F32)],
        compiler_params=_cparams(("parallel", "arbitrary")),
        name="dense_ffn",
    )(x, g.reshape(1, d), mods, mods, mods, w_gu, w_gu, w_down.astype(BF16))


def _router_kernel(n_exp, x_ref, g_ref, sh_ref, sc_ref, r_ref, h_ref, info_ref):
    h = _norm_mod(x_ref[...], g_ref[...], sh_ref[0], sc_ref[0])
    h_ref[...] = h
    logits = jnp.dot(h, r_ref[...], precision=HIGHEST, preferred_element_type=F32)
    lane = lax.broadcasted_iota(jnp.int32, logits.shape, 1).astype(F32)
    logits = jnp.where(lane < n_exp, logits, -jnp.inf)
    m1 = jnp.max(logits, axis=-1, keepdims=True)
    i1 = jnp.min(jnp.where(logits == m1, lane, float(LANES)), axis=-1, keepdims=True)
    rest = jnp.where(lane == i1, -jnp.inf, logits)
    m2 = jnp.max(rest, axis=-1, keepdims=True)
    i2 = jnp.min(jnp.where(rest == m2, lane, float(LANES)), axis=-1, keepdims=True)
    e2 = jnp.exp(m2 - m1)
    w1 = 1.0 / (1.0 + e2)
    w2 = e2 / (1.0 + e2)
    info = jnp.where(lane == 0.0, i1, jnp.where(lane == 1.0, i2, jnp.where(lane == 2.0, w1,
                                                                           jnp.where(lane == 3.0, w2, 0.0))))
    info_ref[...] = info


def _dispatch_kernel(pos_ref, h_ref, xs_in_ref, xs_ref, sem):
    del xs_in_ref
    rows = h_ref.shape[0]

    def issue(r, carry):
        for slot in range(TOP_K):
            pltpu.make_async_copy(h_ref.at[pl.ds(r, 1)], xs_ref.at[pl.ds(pos_ref[0, slot, r], 1)], sem).start()
        return carry

    lax.fori_loop(0, rows, issue, 0)
    for slot in range(TOP_K):
        pltpu.make_async_copy(h_ref, xs_ref.at[pl.ds(0, rows)], sem).wait()


def _grouped_ffn_kernel(te_ref, nu_ref, xs_ref, wg_ref, wu_ref, wd_ref, y_ref, h_scr, acc_scr):
    i, c = pl.program_id(0), pl.program_id(1)
    used = i < nu_ref[0]

    @pl.when(used & (c == 0))
    def _():
        h_scr[...] = xs_ref[...].astype(BF16)
        acc_scr[...] = jnp.zeros_like(acc_scr)

    @pl.when(used)
    def _():
        _swiglu_step(h_scr[...], wg_ref.at[0], wu_ref.at[0], wd_ref.at[0], acc_scr)

    @pl.when(c == pl.num_programs(1) - 1)
    def _():
        y_ref[...] = jnp.where(used, acc_scr[...], 0.0)


def _combine_kernel(pos_ref, x_ref, gate_ref, w_ref, y_ref, o_ref, buf, sem):
    rows = x_ref.shape[0]

    def issue(r, carry):
        for slot in range(TOP_K):
            pltpu.make_async_copy(y_ref.at[pl.ds(pos_ref[0, slot, r], 1)], buf.at[slot, pl.ds(r, 1)], sem).start()
        return carry

    lax.fori_loop(0, rows, issue, 0)
    for slot in range(TOP_K):
        pltpu.make_async_copy(y_ref.at[pl.ds(0, rows)], buf.at[slot], sem).wait()
    mix = w_ref[:, 0:1] * buf[0] + w_ref[:, 1:2] * buf[1]
    o_ref[...] = x_ref[...] + gate_ref[0] * mix


def _moe_ffn(lay, x, g, mods, router, w_gu, w_down, n_tiles):
    d = x.shape[1]
    n_exp, d_ff = w_down.shape[0], w_down.shape[1]
    n_c = d_ff // FF_CHUNK
    n_rows = n_tiles * TM
    tok = lambda i: (i, 0)
    r_pad = jnp.zeros((d, LANES), F32).at[:, :n_exp].set(router)
    h, info = pl.pallas_call(
        functools.partial(_router_kernel, n_exp),
        grid=(n_tiles,),
        in_specs=[pl.BlockSpec((TM, d), tok), pl.BlockSpec((1, d), lambda i: (0, 0)),
                  lay.mod_spec(3, d), lay.mod_spec(4, d), pl.BlockSpec((d, LANES), lambda i: (0, 0))],
        out_specs=[pl.BlockSpec((TM, d), tok), pl.BlockSpec((TM, LANES), tok)],
        out_shape=[jax.ShapeDtypeStruct((n_rows, d), F32), jax.ShapeDtypeStruct((n_rows, LANES), F32)],
        compiler_params=_cparams(("parallel",)),
        name="moe_router",
    )(x, g.reshape(1, d), mods, mods, r_pad)

    expert = info[:, :TOP_K].astype(jnp.int32)
    weight = info[:, TOP_K:2 * TOP_K]
    onehot = (expert.reshape(-1, 1) == jnp.arange(n_exp, dtype=jnp.int32)).astype(jnp.int32)
    csum = jnp.cumsum(onehot, axis=0)
    counts = csum[-1]
    rank = jnp.sum((csum - onehot) * onehot, axis=-1)
    padded = ((counts + MOE_TM - 1) // MOE_TM) * MOE_TM
    ends = jnp.cumsum(padded)
    pos = ((ends - padded)[expert.reshape(-1)] + rank).reshape(n_rows, TOP_K)
    n_rt = (n_rows * TOP_K + n_exp * (MOE_TM - 1)) // MOE_TM
    tile_expert = jnp.minimum(
        jnp.searchsorted(ends, jnp.arange(n_rt, dtype=jnp.int32) * MOE_TM, side="right"), n_exp - 1).astype(jnp.int32)
    n_used = (ends[-1] // MOE_TM).astype(jnp.int32).reshape(1)
    pos_t = pos.reshape(n_tiles, TM, TOP_K).transpose(0, 2, 1)

    xs = pl.pallas_call(
        _dispatch_kernel,
        grid=(n_tiles,),
        in_specs=[pl.BlockSpec((1, TOP_K, TM), lambda i: (i, 0, 0), memory_space=pltpu.SMEM),
                  pl.BlockSpec((TM, d), tok), pl.BlockSpec(memory_space=pl.ANY)],
        out_specs=pl.BlockSpec(memory_space=pl.ANY),
        out_shape=jax.ShapeDtypeStruct((n_rt * MOE_TM, d), F32),
        scratch_shapes=[pltpu.SemaphoreType.DMA(())],
        input_output_aliases={2: 0},
        compiler_params=_cparams(("arbitrary",)),
        name="moe_dispatch",
    )(pos_t, h, jnp.zeros((n_rt * MOE_TM, d), F32))

    w_gu = w_gu.astype(BF16)
    last = n_c - 1
    cc = lambda i, c, nu: jnp.where(i < nu[0], c, last)
    y = pl.pallas_call(
        _grouped_ffn_kernel,
        grid_spec=pltpu.PrefetchScalarGridSpec(
            num_scalar_prefetch=2,
            grid=(n_rt, n_c),
            in_specs=[pl.BlockSpec((MOE_TM, d), lambda i, c, te, nu: (i, 0)),
                      pl.BlockSpec((1, d, FF_CHUNK), lambda i, c, te, nu: (te[i], 0, cc(i, c, nu))),
                      pl.BlockSpec((1, d, FF_CHUNK), lambda i, c, te, nu: (te[i], 0, n_c + cc(i, c, nu))),
                      pl.BlockSpec((1, FF_CHUNK, d), lambda i, c, te, nu: (te[i], cc(i, c, nu), 0))],
            out_specs=pl.BlockSpec((MOE_TM, d), lambda i, c, te, nu: (i, 0)),
            scratch_shapes=[pltpu.VMEM((MOE_TM, d), BF16), pltpu.VMEM((MOE_TM, d), F32)]),
        out_shape=jax.ShapeDtypeStruct((n_rt * MOE_TM, d), F32),
        compiler_params=_cparams(("arbitrary", "arbitrary")),
        name="moe_grouped_ffn",
    )(tile_expert, n_used, xs, w_gu, w_gu, w_down.astype(BF16))

    return pl.pallas_call(
        _combine_kernel,
        grid=(n_tiles,),
        in_specs=[pl.BlockSpec((1, TOP_K, TM), lambda i: (i, 0, 0), memory_space=pltpu.SMEM),
                  pl.BlockSpec((TM, d), tok), lay.mod_spec(5, d), pl.BlockSpec((TM, TOP_K), tok),
                  pl.BlockSpec(memory_space=pl.ANY)],
        out_specs=pl.BlockSpec((TM, d), tok),
        out_shape=jax.ShapeDtypeStruct((n_rows, d), F32),
        scratch_shapes=[pltpu.VMEM((TOP_K, TM, d), F32), pltpu.SemaphoreType.DMA(())],
        compiler_params=_cparams(("arbitrary",)),
        name="moe_combine",
    )(pos_t, x, mods, weight, y)


def _final_norm_kernel(x_ref, g_ref, o_ref):
    x = x_ref[...]
    ms = jnp.mean(x * x, axis=-1, keepdims=True)
    o_ref[...] = (x * lax.rsqrt(ms + EPS)) * g_ref[...]


def _final_norm(lay, x, g):
    d = x.shape[1]
    return pl.pallas_call(
        _final_norm_kernel,
        grid=(lay.lat_tiles,),
        in_specs=[pl.BlockSpec((TM, d), lambda i: (i, 0)), pl.BlockSpec((1, d), lambda i: (0, 0))],
        out_specs=pl.BlockSpec((TM, d), lambda i: (i, 0)),
        out_shape=jax.ShapeDtypeStruct((lay.n_lat, d), F32),
        compiler_params=_cparams(("parallel",)),
        name="final_norm",
    )(x, g.reshape(1, d))


def kernel(x, c, ctx, c_ctx, ada_w, ada_b, norm_g, final_g, lru_w_in, lru_conv_w, lru_conv_b, lru_w_gate, lru_b_gate, lru_lambda, lru_w_out, ret_w_in, ret_log_decay, ret_gn_gain, ret_w_out, swa_w_in, swa_sink, swa_w_out, ffn_w_gu, ffn_w_down, moe_router, moe_w_gu, moe_w_down):
    bsz, seq, d = x.shape
    n_ctx = ctx.shape[1]
    depth = ada_w.shape[0]
    lay = _Layout(bsz, seq, n_ctx)
    xs = jnp.concatenate([x.reshape(-1, d), ctx.reshape(-1, d)], axis=0)
    all_mods = _ada_mods(c, c_ctx, ada_w, ada_b)
    for i in range(depth):
        last = i == depth - 1
        mods = all_mods[i]
        j = i // N_MIXERS
        if i % N_MIXERS == 0:
            xs = _lru_mixer(lay, xs, norm_g[i, 0], mods, lru_w_in[j], lru_conv_w[j], lru_conv_b[j],
                            lru_w_gate[j], lru_b_gate[j], lru_lambda[j], lru_w_out[j])
        elif i % N_MIXERS == 1:
            xs = _ret_mixer(lay, xs, norm_g[i, 0], mods, ret_w_in[j], ret_log_decay[j], ret_gn_gain[j], ret_w_out[j])
        else:
            xs = _swa_mixer(lay, xs, norm_g[i, 0], mods, swa_w_in[j], swa_sink[j], swa_w_out[j])
        n_tiles = lay.lat_tiles if last else lay.n_tiles
        f = i // 2
        if i % 2 == 0:
            xs = _dense_ffn(lay, xs, norm_g[i, 1], mods, ffn_w_gu[f], ffn_w_down[f], n_tiles)
        else:
            xs = _moe_ffn(lay, xs, norm_g[i, 1], mods, moe_router[f], moe_w_gu[f], moe_w_down[f], n_tiles)
    return _final_norm(lay, xs, final_g).reshape(bsz, seq, d)
```

```python
import functools

import jax
import jax.numpy as jnp
from jax import lax
from jax.experimental import pallas as pl
from jax.experimental.pallas import tpu as pltpu

F32 = jnp.float32
BF16 = jnp.bfloat16
HIGHEST = lax.Precision.HIGHEST

EPS = 1e-6
N_MIXERS = 3
GRID_W = 64
LRU_C = 8.0
RET_DK = 256
RET_DV = 512
RET_CHUNK = 128
RET_THETA_BASE = 10000.0
SWA_DH = 64
SWA_BLOCK = 128
ROPE_BASE = 10000.0
TOP_K = 2

LANES = 128
SUBLANES = 8
TM = 512
LRU_TC = 256
FF_CHUNK = 512
MOE_TM = 512
VMEM_LIMIT = 56 * 1024 * 1024
NEG_BIG = -1e30


def _cparams(sem):
    return pltpu.CompilerParams(dimension_semantics=sem, vmem_limit_bytes=VMEM_LIMIT)


def _sigmoid(x):
    return 1.0 / (1.0 + jnp.exp(-x))


def _silu(x):
    return x * _sigmoid(x)


def _gelu_tanh(x):
    return 0.5 * x * (1.0 + jnp.tanh(0.7978845608028654 * (x + 0.044715 * (x * x * x))))


def _norm_mod(x, g, shift, scale):
    ms = jnp.mean(x * x, axis=-1, keepdims=True)
    y = (x * lax.rsqrt(ms + EPS)) * g
    return y * (1.0 + scale) + shift


def _dot(a, b):
    return jnp.dot(a, b, preferred_element_type=F32)


def _dot_nt(a, b):
    return lax.dot_general(a, b, (((1,), (1,)), ((), ())), preferred_element_type=F32)


def _dot_tn(a, b):
    return lax.dot_general(a, b, (((0,), (0,)), ((), ())), preferred_element_type=F32)


class _Layout:
    def __init__(self, bsz, seq, n_ctx):
        self.bsz, self.seq, self.n_ctx = bsz, seq, n_ctx
        self.n_lat = bsz * seq
        self.n_tok = self.n_lat + bsz * n_ctx
        assert seq % TM == 0 and (bsz * n_ctx) % TM == 0
        self.lat_tiles = self.n_lat // TM
        self.tiles_per_batch = seq // TM
        self.n_tiles = self.n_tok // TM

    def mod_row(self, i):
        return jnp.where(i < self.lat_tiles, i // self.tiles_per_batch, self.bsz)

    def pos_tile(self, i):
        return jnp.where(i < self.lat_tiles, i % self.tiles_per_batch, self.tiles_per_batch)

    def mod_spec(self, k, d):
        return pl.BlockSpec((1, 1, d), lambda i, *_: (self.mod_row(i), 0, k))


def _ada_kernel(cond_ref, w_ref, b_ref, o_ref):
    cnd = cond_ref[...]
    o_ref[0] = jnp.dot(_silu(cnd), w_ref[0], precision=HIGHEST, preferred_element_type=F32) + b_ref[0]


def _ada_mods(c, c_ctx, ada_w, ada_b):
    depth, d, d6 = ada_w.shape
    bsz = c.shape[0]
    rows = ((bsz + 1 + SUBLANES - 1) // SUBLANES) * SUBLANES
    cond = jnp.zeros((rows, d), F32).at[:bsz].set(c).at[bsz].set(c_ctx)
    tn = 1536
    assert d6 % tn == 0
    out = pl.pallas_call(
        _ada_kernel,
        grid=(depth, d6 // tn),
        in_specs=[pl.BlockSpec((rows, d), lambda l, n: (0, 0)),
                  pl.BlockSpec((1, d, tn), lambda l, n: (l, 0, n)),
                  pl.BlockSpec((1, 1, tn), lambda l, n: (l, 0, n))],
        out_specs=pl.BlockSpec((1, rows, tn), lambda l, n: (l, 0, n)),
        out_shape=jax.ShapeDtypeStruct((depth, rows, d6), F32),
        compiler_params=_cparams(("parallel", "parallel")),
        name="ada_mods",
    )(cond, ada_w, ada_b.reshape(depth, 1, d6))
    return out.reshape(depth, rows, 1, d6)


def _lru_in_kernel(x_ref, g_ref, sh_ref, sc_ref, w_ref, y_ref, xr_ref):
    width = y_ref.shape[1]
    h = _norm_mod(x_ref[...], g_ref[...], sh_ref[0], sc_ref[0]).astype(BF16)
    y_ref[...] = _dot(h, w_ref[:, :width]).astype(y_ref.dtype)
    xr_ref[...] = _dot(h, w_ref[:, width:]).astype(xr_ref.dtype)


def _lru_in(lay, x, g, mods, w_in):
    d = x.shape[1]
    width = w_in.shape[1] // 2
    return pl.pallas_call(
        _lru_in_kernel,
        grid=(lay.n_tiles,),
        in_specs=[pl.BlockSpec((TM, d), lambda i: (i, 0)),
                  pl.BlockSpec((1, d), lambda i: (0, 0)),
                  lay.mod_spec(0, d), lay.mod_spec(1, d),
                  pl.BlockSpec((d, 2 * width), lambda i: (0, 0))],
        out_specs=[pl.BlockSpec((TM, width), lambda i: (i, 0)),
                   pl.BlockSpec((TM, width), lambda i: (i, 0))],
        out_shape=[jax.ShapeDtypeStruct((lay.n_tok, width), BF16),
                   jax.ShapeDtypeStruct((lay.n_tok, width), F32)],
        compiler_params=_cparams(("parallel",)),
        name="lru_in",
    )(x, g.reshape(1, d), mods, mods, w_in.astype(BF16))


def _lru_chunk_order(lay, reverse):
    n_cc = lay.n_ctx // LRU_TC
    n_lc = lay.seq // LRU_TC

    def order(b, s):
        in_ctx = s < n_cc
        if reverse:
            cidx = jnp.where(in_ctx, n_cc - 1 - s, n_lc - 1 - (s - n_cc))
        else:
            cidx = jnp.where(in_ctx, s, s - n_cc)
        nseq = jnp.where(in_ctx, n_cc, n_lc)
        blk = jnp.where(in_ctx, lay.n_lat // LRU_TC + b * n_cc + cidx, b * n_lc + cidx)
        return cidx, nseq, blk

    return order, n_cc + n_lc


def _lru_scan_body(reverse, order, n_blocks, s, xp_ref, xc_ref, xn_ref, cw_ref, cb_ref, wg_ref, bg_ref,
                   lam_ref, ext_scr, a_scr, b_scr, h_scr, carry_scr):
    tc, width = xc_ref.shape
    bs = width // n_blocks
    cidx, nseq, _ = order(0, s)
    first = cidx == 0
    last = cidx == nseq - 1
    ext_scr[0:SUBLANES, :] = jnp.where(first, 0.0, xp_ref[...])
    ext_scr[SUBLANES:SUBLANES + tc, :] = xc_ref[...]
    ext_scr[SUBLANES + tc:2 * SUBLANES + tc, :] = jnp.where(last, 0.0, xn_ref[...])
    n_taps = cw_ref.shape[0]
    left = n_taps // 2
    conv = cb_ref[...] + sum(cw_ref[k:k + 1, :] * ext_scr[pl.ds(SUBLANES - left + k, tc), :]
                             for k in range(n_taps))
    xb = conv.astype(BF16)
    for n in range(n_blocks):
        cols = slice(n * bs, (n + 1) * bs)
        gates = _dot(xb[:, cols], wg_ref[n])
        r = _sigmoid(gates[:, :bs] + bg_ref[0:1, cols])
        i = _sigmoid(gates[:, bs:] + bg_ref[1:2, cols])
        nl = -lam_ref[:, cols]
        softplus = jnp.maximum(nl, 0.0) + jnp.log1p(jnp.exp(-jnp.abs(nl)))
        log_a = -LRU_C * r * softplus
        a = jnp.exp(log_a)
        a_scr[:, cols] = a
        b_scr[:, cols] = jnp.sqrt(1.0 - a * a) * (i * conv[:, cols])

    @pl.when(s == 0)
    def _():
        carry_scr[...] = jnp.zeros_like(carry_scr)

    def step(t, h):
        tt = tc - 1 - t if reverse else t
        h = a_scr[pl.ds(tt, 1), :] * h + b_scr[pl.ds(tt, 1), :]
        h_scr[pl.ds(tt, 1), :] = h
        return h

    carry_scr[...] = lax.fori_loop(0, tc, step, carry_scr[...], unroll=8)


def _lru_fwd_kernel(order, n_blocks, xp_ref, xc_ref, xn_ref, cw_ref, cb_ref, wg_ref, bg_ref, lam_ref,
                    hf_ref, ext_scr, a_scr, b_scr, h_scr, carry_scr):
    s = pl.program_id(1)
    _lru_scan_body(False, order, n_blocks, s, xp_ref, xc_ref, xn_ref, cw_ref, cb_ref, wg_ref, bg_ref,
                   lam_ref, ext_scr, a_scr, b_scr, h_scr, carry_scr)
    hf_ref[...] = h_scr[...].astype(hf_ref.dtype)


def _lru_bwd_kernel(order, n_blocks, xp_ref, xc_ref, xn_ref, cw_ref, cb_ref, wg_ref, bg_ref, lam_ref,
                    hf_ref, y_ref, x_ref, gate_ref, wo_ref, o_ref, ext_scr, a_scr, b_scr, h_scr, carry_scr):
    s = pl.program_id(1)
    _lru_scan_body(True, order, n_blocks, s, xp_ref, xc_ref, xn_ref, cw_ref, cb_ref, wg_ref, bg_ref,
                   lam_ref, ext_scr, a_scr, b_scr, h_scr, carry_scr)
    hsum = h_scr[...] + hf_ref[...].astype(F32)
    z = (_gelu_tanh(y_ref[...].astype(F32)) * hsum).astype(BF16)
    o_ref[...] = x_ref[...] + gate_ref[0] * _dot(z, wo_ref[...])


def _lru_mixer(lay, x, g, mods, w_in, conv_w, conv_b, w_gate, b_gate, lam, w_out):
    d = x.shape[1]
    y, xr = _lru_in(lay, x, g, mods, w_in)
    width = xr.shape[1]
    n_blocks, bs = w_gate.shape[2], w_gate.shape[3]
    tc = LRU_TC
    assert lay.seq % tc == 0 and lay.n_ctx % tc == 0
    halo_per_chunk = tc // SUBLANES
    n_halo = lay.n_tok // SUBLANES
    wg = jnp.concatenate([w_gate[:, 0], w_gate[:, 1]], axis=-1).astype(BF16)
    cb = conv_b.reshape(1, width)
    lam = lam.reshape(2, 1, width)
    scratch = [pltpu.VMEM((tc + 2 * SUBLANES, width), F32), pltpu.VMEM((tc, width), F32),
               pltpu.VMEM((tc, width), F32), pltpu.VMEM((tc, width), F32), pltpu.VMEM((1, width), F32)]

    def specs(order):
        blk = lambda b, s: order(b, s)[2]
        return [pl.BlockSpec((SUBLANES, width), lambda b, s: (jnp.maximum(blk(b, s) * halo_per_chunk - 1, 0), 0)),
                pl.BlockSpec((tc, width), lambda b, s: (blk(b, s), 0)),
                pl.BlockSpec((SUBLANES, width),
                             lambda b, s: (jnp.minimum((blk(b, s) + 1) * halo_per_chunk, n_halo - 1), 0)),
                pl.BlockSpec(conv_w.shape, lambda b, s: (0, 0)),
                pl.BlockSpec((1, width), lambda b, s: (0, 0))]

    def dir_specs(dr):
        return [pl.BlockSpec((None, n_blocks, bs, 2 * bs), lambda b, s: (dr, 0, 0, 0)),
                pl.BlockSpec((None, 2, width), lambda b, s: (dr, 0, 0)),
                pl.BlockSpec((None, 1, width), lambda b, s: (dr, 0, 0))]

    order_f, n_steps = _lru_chunk_order(lay, False)
    hf = pl.pallas_call(
        functools.partial(_lru_fwd_kernel, order_f, n_blocks),
        grid=(lay.bsz, n_steps),
        in_specs=specs(order_f) + dir_specs(0),
        out_specs=pl.BlockSpec((tc, width), lambda b, s: (order_f(b, s)[2], 0)),
        out_shape=jax.ShapeDtypeStruct((lay.n_tok, width), BF16),
        scratch_shapes=scratch,
        compiler_params=_cparams(("parallel", "arbitrary")),
        name="lru_scan_fwd",
    )(xr, xr, xr, conv_w, cb, wg, b_gate, lam)

    order_b, _ = _lru_chunk_order(lay, True)
    row = lambda b, s: (order_b(b, s)[2], 0)
    gate_row = lambda b, s: (jnp.where(s < lay.n_ctx // tc, lay.bsz, b), 0, 2)
    return pl.pallas_call(
        functools.partial(_lru_bwd_kernel, order_b, n_blocks),
        grid=(lay.bsz, n_steps),
        in_specs=specs(order_b) + dir_specs(1) + [
            pl.BlockSpec((tc, width), row), pl.BlockSpec((tc, width), row), pl.BlockSpec((tc, d), row),
            pl.BlockSpec((1, 1, d), gate_row), pl.BlockSpec((width, d), lambda b, s: (0, 0))],
        out_specs=pl.BlockSpec((tc, d), row),
        out_shape=jax.ShapeDtypeStruct((lay.n_tok, d), F32),
        scratch_shapes=scratch,
        compiler_params=_cparams(("parallel", "arbitrary")),
        name="lru_scan_bwd_out",
    )(xr, xr, xr, conv_w, cb, wg, b_gate, lam, hf, y, x, mods, w_out.astype(BF16))


def _ret_in_kernel(heads, x_ref, g_ref, sh_ref, sc_ref, w_ref, cos_ref, sin_ref, q_ref, k_ref, v_ref, gt_ref):
    h = _norm_mod(x_ref[...], g_ref[...], sh_ref[0], sc_ref[0]).astype(BF16)
    dk, half = RET_DK, RET_DK // 2
    dkt = heads * dk
    cos, sin = cos_ref[...], sin_ref[...]
    for which, out, scale in ((0, q_ref, 1.0), (1, k_ref, RET_DK ** -0.5)):
        for hd in range(heads):
            c0 = which * dkt + hd * dk
            r = _dot(h, w_ref[:, c0:c0 + dk]) * scale
            x1, x2 = r[:, :half], r[:, half:]
            out[:, hd * dk:hd * dk + half] = (x1 * cos - x2 * sin).astype(out.dtype)
            out[:, hd * dk + half:(hd + 1) * dk] = (x2 * cos + x1 * sin).astype(out.dtype)
    dvt = v_ref.shape[1]
    v_ref[...] = _dot(h, w_ref[:, 2 * dkt:2 * dkt + dvt]).astype(v_ref.dtype)
    gt_ref[...] = _dot(h, w_ref[:, 2 * dkt + dvt:]).astype(gt_ref.dtype)


def _ret_tables(lay):
    n_freq = RET_DK // 2
    theta = RET_THETA_BASE ** (-jnp.arange(n_freq, dtype=F32) / n_freq)
    ang = jnp.arange(lay.seq, dtype=F32)[:, None] * theta
    cos = jnp.concatenate([jnp.cos(ang), jnp.ones((TM, n_freq), F32)], axis=0)
    sin = jnp.concatenate([jnp.sin(ang), jnp.zeros((TM, n_freq), F32)], axis=0)
    return cos, sin


def _ret_in(lay, x, g, mods, w_in, heads):
    d = x.shape[1]
    dkt, dvt = heads * RET_DK, heads * RET_DV
    cos, sin = _ret_tables(lay)
    half = RET_DK // 2
    tok = lambda i: (i, 0)
    return pl.pallas_call(
        functools.partial(_ret_in_kernel, heads),
        grid=(lay.n_tiles,),
        in_specs=[pl.BlockSpec((TM, d), tok), pl.BlockSpec((1, d), lambda i: (0, 0)),
                  lay.mod_spec(0, d), lay.mod_spec(1, d),
                  pl.BlockSpec(w_in.shape, lambda i: (0, 0)),
                  pl.BlockSpec((TM, half), lambda i: (lay.pos_tile(i), 0)),
                  pl.BlockSpec((TM, half), lambda i: (lay.pos_tile(i), 0))],
        out_specs=[pl.BlockSpec((TM, dkt), tok), pl.BlockSpec((TM, dkt), tok),
                   pl.BlockSpec((TM, dvt), tok), pl.BlockSpec((TM, dvt), tok)],
        out_shape=[jax.ShapeDtypeStruct((lay.n_tok, dkt), BF16), jax.ShapeDtypeStruct((lay.n_tok, dkt), BF16),
                   jax.ShapeDtypeStruct((lay.n_tok, dvt), BF16), jax.ShapeDtypeStruct((lay.n_tok, dvt), BF16)],
        compiler_params=_cparams(("parallel",)),
        name="ret_in",
    )(x, g.reshape(1, d), mods, mods, w_in.astype(BF16), cos, sin)


def _ret_chunk_order(lay, reverse):
    n_cc = lay.n_ctx // RET_CHUNK
    n_lc = lay.seq // RET_CHUNK

    def blk(b, s):
        in_ctx = s < n_cc
        if reverse:
            cidx = jnp.where(in_ctx, n_cc - 1 - s, n_lc - 1 - (s - n_cc))
        else:
            cidx = jnp.where(in_ctx, s, s - n_cc)
        return jnp.where(in_ctx, lay.n_lat // RET_CHUNK + b * n_cc + cidx, b * n_lc + cidx)

    return blk, n_cc + n_lc


def _ret_chunk(reverse, heads, ld_ref, q_ref, k_ref, v_ref, state_scr, emit):
    c = RET_CHUNK
    dk, dv = RET_DK, RET_DV
    dr = 1 if reverse else 0

    @pl.when(pl.program_id(1) == 0)
    def _():
        state_scr[...] = jnp.zeros_like(state_scr)

    ri = lax.broadcasted_iota(jnp.int32, (c, c), 0)
    ci = lax.broadcasted_iota(jnp.int32, (c, c), 1)
    diff = (ci - ri if reverse else ri - ci).astype(F32)
    pos = lax.broadcasted_iota(jnp.int32, (c, dk), 0).astype(F32)
    for h in range(heads):
        lg = ld_ref[dr, h]
        intra = jnp.where(diff >= 0, jnp.exp(lg * jnp.maximum(diff, 0.0)), 0.0)
        if reverse:
            q_dec, k_dec = jnp.exp(lg * (c - pos)), jnp.exp(lg * pos)
        else:
            q_dec, k_dec = jnp.exp(lg * (pos + 1.0)), jnp.exp(lg * (c - 1.0 - pos))
        c_dec = jnp.exp(jnp.full((1, dv), lg, F32) * float(c))
        qh = q_ref[:, h * dk:(h + 1) * dk]
        kh = k_ref[:, h * dk:(h + 1) * dk]
        vh = v_ref[:, h * dv:(h + 1) * dv]
        sc = _dot_nt(qh, kh) * intra
        st = state_scr[h]
        o = _dot(sc.astype(BF16), vh) + _dot((qh.astype(F32) * q_dec).astype(BF16), st.astype(BF16))
        kd = (kh.astype(F32) * k_dec).astype(BF16)
        state_scr[h] = c_dec * st + _dot_tn(kd, vh)
        emit(h, o)


def _ret_fwd_kernel(heads, ld_ref, q_ref, k_ref, v_ref, of_ref, state_scr):
    def emit(h, o):
        of_ref[:, h * RET_DV:(h + 1) * RET_DV] = o.astype(of_ref.dtype)

    _ret_chunk(False, heads, ld_ref, q_ref, k_ref, v_ref, state_scr, emit)


def _ret_bwd_kernel(heads, ld_ref, q_ref, k_ref, v_ref, of_ref, gt_ref, gain_ref, z_ref, state_scr):
    def emit(h, o):
        cols = slice(h * RET_DV, (h + 1) * RET_DV)
        osum = o + of_ref[:, cols].astype(F32)
        on = osum * lax.rsqrt(jnp.mean(osum * osum, axis=-1, keepdims=True) + EPS)
        z_ref[:, cols] = (_silu(gt_ref[:, cols].astype(F32)) * (on * gain_ref[:, cols])).astype(z_ref.dtype)

    _ret_chunk(True, heads, ld_ref, q_ref, k_ref, v_ref, state_scr, emit)


def _ret_mixer(lay, x, g, mods, w_in, log_decay, gn_gain, w_out):
    heads = log_decay.shape[1]
    dkt, dvt = heads * RET_DK, heads * RET_DV
    q, k, v, gt = _ret_in(lay, x, g, mods, w_in, heads)
    c = RET_CHUNK
    assert lay.seq % c == 0 and lay.n_ctx % c == 0
    state = [pltpu.VMEM((heads, RET_DK, RET_DV), F32)]
    smem = pl.BlockSpec(memory_space=pltpu.SMEM)

    blk_f, n_steps = _ret_chunk_order(lay, False)
    row_f = lambda b, s: (blk_f(b, s), 0)
    of = pl.pallas_call(
        functools.partial(_ret_fwd_kernel, heads),
        grid=(lay.bsz, n_steps),
        in_specs=[smem, pl.BlockSpec((c, dkt), row_f), pl.BlockSpec((c, dkt), row_f), pl.BlockSpec((c, dvt), row_f)],
        out_specs=pl.BlockSpec((c, dvt), row_f),
        out_shape=jax.ShapeDtypeStruct((lay.n_tok, dvt), BF16),
        scratch_shapes=state,
        compiler_params=_cparams(("parallel", "arbitrary")),
        name="ret_fwd",
    )(log_decay, q, k, v)

    blk_b, _ = _ret_chunk_order(lay, True)
    row_b = lambda b, s: (blk_b(b, s), 0)
    z = pl.pallas_call(
        functools.partial(_ret_bwd_kernel, heads),
        grid=(lay.bsz, n_steps),
        in_specs=[smem, pl.BlockSpec((c, dkt), row_b), pl.BlockSpec((c, dkt), row_b), pl.BlockSpec((c, dvt), row_b),
                  pl.BlockSpec((c, dvt), row_b), pl.BlockSpec((c, dvt), row_b),
                  pl.BlockSpec((1, dvt), lambda b, s: (0, 0))],
        out_specs=pl.BlockSpec((c, dvt), row_b),
        out_shape=jax.ShapeDtypeStruct((lay.n_tok, dvt), BF16),
        scratch_shapes=state,
        compiler_params=_cparams(("parallel", "arbitrary")),
        name="ret_bwd_norm",
    )(log_decay, q, k, v, of, gt, gn_gain.reshape(1, dvt))
    return _out_proj(lay, z, w_out, x, mods, 2)


def _swa_in_kernel(n_rot, x_ref, g_ref, sh_ref, sc_ref, w_ref, cos_ref, sin_ref, q_ref, k_ref, v_ref):
    h = _norm_mod(x_ref[...], g_ref[...], sh_ref[0], sc_ref[0]).astype(BF16)
    cos, sin = cos_ref[...], sin_ref[...]
    lane = lax.broadcasted_iota(jnp.int32, cos.shape, 1)
    low = (lane % SWA_DH) < SWA_DH // 2
    nq = q_ref.shape[1]
    for c0 in range(0, n_rot, LANES):
        r = _dot(h, w_ref[:, c0:c0 + LANES])
        partner = jnp.where(low, pltpu.roll(r, LANES - SWA_DH // 2, 1), pltpu.roll(r, SWA_DH // 2, 1))
        rot = r * cos + partner * sin
        if c0 < nq:
            q_ref[:, c0:c0 + LANES] = (rot * (SWA_DH ** -0.5)).astype(q_ref.dtype)
        else:
            k_ref[:, c0 - nq:c0 - nq + LANES] = rot.astype(k_ref.dtype)
    v_ref[...] = _dot(h, w_ref[:, n_rot:]).astype(v_ref.dtype)


def _swa_tables(lay):
    length = lay.seq
    rows = length // GRID_W
    row = jnp.repeat(jnp.arange(rows, dtype=F32), GRID_W)
    col = jnp.tile(jnp.arange(GRID_W, dtype=F32), rows)
    n_freq = SWA_DH // 4
    freq = ROPE_BASE ** (-jnp.arange(n_freq, dtype=F32) / n_freq)
    ang = jnp.concatenate([row[:, None] * freq, col[:, None] * freq], axis=-1)
    reps = LANES // (SWA_DH // 2)
    cos = jnp.tile(jnp.cos(ang), (1, reps))
    sign = jnp.tile(jnp.concatenate([-jnp.ones((SWA_DH // 2,), F32), jnp.ones((SWA_DH // 2,), F32)]), LANES // SWA_DH)
    sin = jnp.tile(jnp.sin(ang), (1, reps)) * sign
    cos = jnp.concatenate([cos, jnp.ones((TM, LANES), F32)], axis=0)
    sin = jnp.concatenate([sin, jnp.zeros((TM, LANES), F32)], axis=0)
    return cos, sin


def _swa_in(lay, x, g, mods, w_in, hq, hkv):
    d = x.shape[1]
    nq, nkv = hq * SWA_DH, hkv * SWA_DH
    cos, sin = _swa_tables(lay)
    tok = lambda i: (i, 0)
    return pl.pallas_call(
        functools.partial(_swa_in_kernel, nq + nkv),
        grid=(lay.n_tiles,),
        in_specs=[pl.BlockSpec((TM, d), tok), pl.BlockSpec((1, d), lambda i: (0, 0)),
                  lay.mod_spec(0, d), lay.mod_spec(1, d),
                  pl.BlockSpec(w_in.shape, lambda i: (0, 0)),
                  pl.BlockSpec((TM, LANES), lambda i: (lay.pos_tile(i), 0)),
                  pl.BlockSpec((TM, LANES), lambda i: (lay.pos_tile(i), 0))],
        out_specs=[pl.BlockSpec((TM, nq), tok), pl.BlockSpec((TM, nkv), tok), pl.BlockSpec((TM, nkv), tok)],
        out_shape=[jax.ShapeDtypeStruct((lay.n_tok, nq), BF16), jax.ShapeDtypeStruct((lay.n_tok, nkv), BF16),
                   jax.ShapeDtypeStruct((lay.n_tok, nkv), BF16)],
        compiler_params=_cparams(("parallel",)),
        name="swa_in",
    )(x, g.reshape(1, d), mods, mods, w_in.astype(BF16), cos, sin)


def _attend_group(hk, groups, sink_ref, q_ref, kh, vh, bias, o_ref):
    blk, dh = q_ref.shape[0], SWA_DH
    heads = [hk * groups + g for g in range(groups)]
    qg = jnp.concatenate([q_ref[:, h * dh:(h + 1) * dh] for h in heads], axis=0)
    s = _dot_nt(qg, kh)
    probs, inv = [], []
    for g, h in enumerate(heads):
        sg = s[g * blk:(g + 1) * blk]
        if bias is not None:
            sg = sg + bias
        sink = sink_ref[h]
        m = jnp.maximum(jnp.max(sg, axis=-1, keepdims=True), sink)
        p = jnp.exp(sg - m)
        inv.append(1.0 / (jnp.sum(p, axis=-1, keepdims=True) + jnp.exp(sink - m)))
        probs.append(p.astype(BF16))
    o = _dot(jnp.concatenate(probs, axis=0), vh)
    for g, h in enumerate(heads):
        o_ref[:, h * dh:(h + 1) * dh] = (o[g * blk:(g + 1) * blk] * inv[g]).astype(o_ref.dtype)


def _swa_attn_kernel(n_qb, hq, hkv, sink_ref, q_ref, kp_ref, kc_ref, kn_ref, kx_ref, vp_ref, vc_ref, vn_ref,
                     vx_ref, o_ref, k_scr, v_scr):
    j = pl.program_id(1)
    blk = SWA_BLOCK
    n_ctx = kx_ref.shape[0]
    groups = hq // hkv
    dh = SWA_DH

    @pl.when(j < n_qb)
    def _():
        k_scr[0:blk, :] = kp_ref[...]
        k_scr[blk:2 * blk, :] = kc_ref[...]
        k_scr[2 * blk:3 * blk, :] = kn_ref[...]
        k_scr[3 * blk:, :] = kx_ref[...]
        v_scr[0:blk, :] = vp_ref[...]
        v_scr[blk:2 * blk, :] = vc_ref[...]
        v_scr[2 * blk:3 * blk, :] = vn_ref[...]
        v_scr[3 * blk:, :] = vx_ref[...]
        nk = 3 * blk + n_ctx
        r = lax.broadcasted_iota(jnp.int32, (blk, nk), 0)
        c = lax.broadcasted_iota(jnp.int32, (blk, nk), 1)
        lo = jnp.maximum(r, jnp.where(j > 0, 0, blk))
        hi = jnp.minimum(r + 2 * blk, jnp.where(j < n_qb - 1, 3 * blk, 2 * blk) - 1)
        bias = jnp.where(c >= 3 * blk, 0.0, jnp.where(c < lo, NEG_BIG, jnp.where(c > hi, NEG_BIG, 0.0)))
        for hk in range(hkv):
            _attend_group(hk, groups, sink_ref, q_ref, k_scr[:, hk * dh:(hk + 1) * dh],
                          v_scr[:, hk * dh:(hk + 1) * dh], bias, o_ref)

    @pl.when(j >= n_qb)
    def _():
        for hk in range(hkv):
            _attend_group(hk, groups, sink_ref, q_ref, kx_ref[:, hk * dh:(hk + 1) * dh],
                          vx_ref[:, hk * dh:(hk + 1) * dh], None, o_ref)


def _swa_mixer(lay, x, g, mods, w_in, sink, w_out):
    hq = sink.shape[0]
    hkv = (w_in.shape[1] // SWA_DH - hq) // 2
    nq, nkv = hq * SWA_DH, hkv * SWA_DH
    q, k, v = _swa_in(lay, x, g, mods, w_in, hq, hkv)
    blk = SWA_BLOCK
    n_qb, n_cb = lay.seq // blk, lay.n_ctx // blk
    assert lay.seq % blk == 0 and lay.n_ctx % blk == 0
    ctx_blk0 = lay.n_lat // blk

    def qrow(b, j):
        return (jnp.where(j < n_qb, b * n_qb + j, ctx_blk0 + b * n_cb + (j - n_qb)), 0)

    def band(off):
        return lambda b, j: (b * n_qb + jnp.clip(jnp.minimum(j, n_qb - 1) + off, 0, n_qb - 1), 0)

    ctx_row = lambda b, j: (lay.n_lat // lay.n_ctx + b, 0)
    kv_specs = [pl.BlockSpec((blk, nkv), band(-1)), pl.BlockSpec((blk, nkv), band(0)),
                pl.BlockSpec((blk, nkv), band(1)), pl.BlockSpec((lay.n_ctx, nkv), ctx_row)]
    o = pl.pallas_call(
        functools.partial(_swa_attn_kernel, n_qb, hq, hkv),
        grid=(lay.bsz, n_qb + n_cb),
        in_specs=[pl.BlockSpec(memory_space=pltpu.SMEM), pl.BlockSpec((blk, nq), qrow)] + kv_specs + kv_specs,
        out_specs=pl.BlockSpec((blk, nq), qrow),
        out_shape=jax.ShapeDtypeStruct((lay.n_tok, nq), BF16),
        scratch_shapes=[pltpu.VMEM((3 * blk + lay.n_ctx, nkv), BF16), pltpu.VMEM((3 * blk + lay.n_ctx, nkv), BF16)],
        compiler_params=_cparams(("parallel", "arbitrary")),
        name="swa_attn",
    )(sink, q, k, k, k, k, v, v, v, v)
    return _out_proj(lay, o, w_out, x, mods, 2)


def _out_proj_kernel(a_ref, w_ref, x_ref, gate_ref, o_ref):
    o_ref[...] = x_ref[...] + gate_ref[0] * _dot(a_ref[...], w_ref[...])


def _out_proj(lay, a, w, x, mods, gate_idx):
    d = x.shape[1]
    kdim = a.shape[1]
    tok = lambda i: (i, 0)
    return pl.pallas_call(
        _out_proj_kernel,
        grid=(lay.n_tiles,),
        in_specs=[pl.BlockSpec((TM, kdim), tok), pl.BlockSpec((kdim, d), lambda i: (0, 0)),
                  pl.BlockSpec((TM, d), tok), lay.mod_spec(gate_idx, d)],
        out_specs=pl.BlockSpec((TM, d), tok),
        out_shape=jax.ShapeDtypeStruct((lay.n_tok, d), F32),
        compiler_params=_cparams(("parallel",)),
        name="out_proj",
    )(a, w.astype(BF16), x, mods)


def _load_swiglu_weights(wgu_hbm, wd_hbm, wgu_scr, wd_scr, stg_gu, stg_d, sems):
    n_c = wd_scr.shape[0]
    ff = wd_scr.shape[1]

    def copies(c):
        return (pltpu.make_async_copy(wgu_hbm.at[:, pl.ds(c * ff, ff)], stg_gu.at[0], sems.at[0]),
                pltpu.make_async_copy(wgu_hbm.at[:, pl.ds((n_c + c) * ff, ff)], stg_gu.at[1], sems.at[1]),
                pltpu.make_async_copy(wd_hbm.at[pl.ds(c * ff, ff), :], stg_d.at[c % 2], sems.at[2 + c % 2]))

    for cp in copies(0):
        cp.start()
    for c in range(n_c):
        cp_g, cp_u, cp_d = copies(c)
        cp_g.wait()
        wgu_scr[c] = stg_gu[0].astype(BF16)
        cp_u.wait()
        wgu_scr[n_c + c] = stg_gu[1].astype(BF16)
        if c + 1 < n_c:
            for cp in copies(c + 1):
                cp.start()
        cp_d.wait()
        wd_scr[c] = stg_d[c % 2].astype(BF16)


def _swiglu_resident(h, wgu_scr, wd_scr, acc_scr):
    n_c = wd_scr.shape[0]
    for c in range(n_c):
        gg = _dot(h, wgu_scr[c])
        uu = _dot(h, wgu_scr[n_c + c])
        contrib = _dot((_silu(gg) * uu).astype(BF16), wd_scr[c])
        if c == 0:
            acc_scr[...] = contrib
        else:
            acc_scr[...] += contrib


def _ffn_kernel(layer, x_ref, g_ref, sh_ref, sc_ref, gate_ref, wgu_hbm, wd_hbm, o_ref,
                wgu_scr, wd_scr, stg_gu, stg_d, sems):
    @pl.when(pl.program_id(0) == 0)
    def _():
        _load_swiglu_weights(wgu_hbm.at[layer], wd_hbm.at[layer], wgu_scr, wd_scr, stg_gu, stg_d, sems)

    h = _norm_mod(x_ref[...], g_ref[...], sh_ref[0], sc_ref[0]).astype(BF16)
    _swiglu_resident(h, wgu_scr, wd_scr, o_ref)
    o_ref[...] = x_ref[...] + gate_ref[0] * o_ref[...]


def _swiglu_scratch(d, d_ff):
    n_c = d_ff // FF_CHUNK
    return [pltpu.VMEM((2 * n_c, d, FF_CHUNK), BF16), pltpu.VMEM((n_c, FF_CHUNK, d), BF16),
            pltpu.VMEM((2, d, FF_CHUNK), F32), pltpu.VMEM((2, FF_CHUNK, d), F32),
            pltpu.SemaphoreType.DMA((4,))]


def _dense_ffn(lay, x, g, mods, w_gu_all, w_down_all, layer, n_tiles):
    d = x.shape[1]
    d_ff = w_down_all.shape[1]
    assert d_ff % FF_CHUNK == 0
    tok = lambda i: (i, 0)
    return pl.pallas_call(
        functools.partial(_ffn_kernel, layer),
        grid=(n_tiles,),
        in_specs=[pl.BlockSpec((TM, d), tok), pl.BlockSpec((1, d), lambda i: (0, 0)),
                  lay.mod_spec(3, d), lay.mod_spec(4, d), lay.mod_spec(5, d),
                  pl.BlockSpec(memory_space=pl.ANY), pl.BlockSpec(memory_space=pl.ANY)],
        out_specs=pl.BlockSpec((TM, d), tok),
        out_shape=jax.ShapeDtypeStruct((n_tiles * TM, d), F32),
        scratch_shapes=_swiglu_scratch(d, d_ff),
        compiler_params=_cparams(("arbitrary",)),
        name="dense_ffn",
    )(x, g.reshape(1, d), mods, mods, mods, w_gu_all, w_down_all)


def _router_kernel(n_exp, x_ref, g_ref, sh_ref, sc_ref, r_ref, h_ref, info_ref):
    h = _norm_mod(x_ref[...], g_ref[...], sh_ref[0], sc_ref[0])
    h_ref[...] = h
    logits = jnp.dot(h, r_ref[...], precision=HIGHEST, preferred_element_type=F32)
    lane = lax.broadcasted_iota(jnp.int32, logits.shape, 1).astype(F32)
    logits = jnp.where(lane < n_exp, logits, -jnp.inf)
    m1 = jnp.max(logits, axis=-1, keepdims=True)
    i1 = jnp.min(jnp.where(logits == m1, lane, float(LANES)), axis=-1, keepdims=True)
    rest = jnp.where(lane == i1, -jnp.inf, logits)
    m2 = jnp.max(rest, axis=-1, keepdims=True)
    i2 = jnp.min(jnp.where(rest == m2, lane, float(LANES)), axis=-1, keepdims=True)
    e2 = jnp.exp(m2 - m1)
    w1 = 1.0 / (1.0 + e2)
    w2 = e2 / (1.0 + e2)
    info = jnp.where(lane == 0.0, i1, jnp.where(lane == 1.0, i2, jnp.where(lane == 2.0, w1,
                                                                           jnp.where(lane == 3.0, w2, 0.0))))
    info_ref[...] = info


def _dispatch_kernel(pos_ref, h_ref, xs_in_ref, xs_ref, sem):
    del xs_in_ref
    rows = h_ref.shape[0]

    def issue(r, carry):
        for slot in range(TOP_K):
            pltpu.make_async_copy(h_ref.at[pl.ds(r, 1)], xs_ref.at[pl.ds(pos_ref[0, slot, r], 1)], sem).start()
        return carry

    lax.fori_loop(0, rows, issue, 0, unroll=8)
    for slot in range(TOP_K):
        pltpu.make_async_copy(h_ref, xs_ref.at[pl.ds(0, rows)], sem).wait()


def _grouped_ffn_kernel(layer, te_ref, nu_ref, xs_ref, wgu_hbm, wd_hbm, y_ref,
                        wgu_scr, wd_scr, stg_gu, stg_d, sems):
    i = pl.program_id(0)
    used = i < nu_ref[0]
    expert = te_ref[i]
    new_expert = (i == 0) | (expert != te_ref[jnp.maximum(i - 1, 0)])

    @pl.when(used & new_expert)
    def _():
        _load_swiglu_weights(wgu_hbm.at[layer, expert], wd_hbm.at[layer, expert], wgu_scr, wd_scr,
                             stg_gu, stg_d, sems)

    @pl.when(used)
    def _():
        _swiglu_resident(xs_ref[...].astype(BF16), wgu_scr, wd_scr, y_ref)

    @pl.when(jnp.logical_not(used))
    def _():
        y_ref[...] = jnp.zeros_like(y_ref)


def _combine_kernel(pos_ref, x_ref, gate_ref, w_ref, y_ref, o_ref, buf, sem):
    rows = x_ref.shape[0]

    def issue(r, carry):
        for slot in range(TOP_K):
            pltpu.make_async_copy(y_ref.at[pl.ds(pos_ref[0, slot, r], 1)], buf.at[slot, pl.ds(r, 1)], sem).start()
        return carry

    lax.fori_loop(0, rows, issue, 0, unroll=8)
    for slot in range(TOP_K):
        pltpu.make_async_copy(y_ref.at[pl.ds(0, rows)], buf.at[slot], sem).wait()
    mix = w_ref[:, 0:1] * buf[0] + w_ref[:, 1:2] * buf[1]
    o_ref[...] = x_ref[...] + gate_ref[0] * mix


def _moe_ffn(lay, x, g, mods, router, w_gu_all, w_down_all, layer, n_tiles):
    d = x.shape[1]
    n_exp, d_ff = w_down_all.shape[1], w_down_all.shape[2]
    assert d_ff % FF_CHUNK == 0
    n_rows = n_tiles * TM
    tok = lambda i: (i, 0)
    r_pad = jnp.zeros((d, LANES), F32).at[:, :n_exp].set(router)
    h, info = pl.pallas_call(
        functools.partial(_router_kernel, n_exp),
        grid=(n_tiles,),
        in_specs=[pl.BlockSpec((TM, d), tok), pl.BlockSpec((1, d), lambda i: (0, 0)),
                  lay.mod_spec(3, d), lay.mod_spec(4, d), pl.BlockSpec((d, LANES), lambda i: (0, 0))],
        out_specs=[pl.BlockSpec((TM, d), tok), pl.BlockSpec((TM, LANES), tok)],
        out_shape=[jax.ShapeDtypeStruct((n_rows, d), F32), jax.ShapeDtypeStruct((n_rows, LANES), F32)],
        compiler_params=_cparams(("parallel",)),
        name="moe_router",
    )(x, g.reshape(1, d), mods, mods, r_pad)

    expert = info[:, :TOP_K].astype(jnp.int32)
    weight = info[:, TOP_K:2 * TOP_K]
    onehot = (expert.reshape(-1, 1) == jnp.arange(n_exp, dtype=jnp.int32)).astype(jnp.int32)
    csum = jnp.cumsum(onehot, axis=0)
    counts = csum[-1]
    rank = jnp.sum((csum - onehot) * onehot, axis=-1)
    padded = ((counts + MOE_TM - 1) // MOE_TM) * MOE_TM
    ends = jnp.cumsum(padded)
    pos = ((ends - padded)[expert.reshape(-1)] + rank).reshape(n_rows, TOP_K)
    n_rt = (n_rows * TOP_K + n_exp * (MOE_TM - 1)) // MOE_TM
    tile_start = jnp.arange(n_rt, dtype=jnp.int32) * MOE_TM
    tile_expert = jnp.minimum(jnp.sum((ends[None, :] <= tile_start[:, None]).astype(jnp.int32), axis=1), n_exp - 1)
    n_used = (ends[-1] // MOE_TM).astype(jnp.int32).reshape(1)
    pos_t = pos.reshape(n_tiles, TM, TOP_K).transpose(0, 2, 1)

    xs = pl.pallas_call(
        _dispatch_kernel,
        grid=(n_tiles,),
        in_specs=[pl.BlockSpec((1, TOP_K, TM), lambda i: (i, 0, 0), memory_space=pltpu.SMEM),
                  pl.BlockSpec((TM, d), tok), pl.BlockSpec(memory_space=pl.ANY)],
        out_specs=pl.BlockSpec(memory_space=pl.ANY),
        out_shape=jax.ShapeDtypeStruct((n_rt * MOE_TM, d), F32),
        scratch_shapes=[pltpu.SemaphoreType.DMA(())],
        input_output_aliases={2: 0},
        compiler_params=_cparams(("arbitrary",)),
        name="moe_dispatch",
    )(pos_t, h, jnp.zeros((n_rt * MOE_TM, d), F32))

    y = pl.pallas_call(
        functools.partial(_grouped_ffn_kernel, layer),
        grid_spec=pltpu.PrefetchScalarGridSpec(
            num_scalar_prefetch=2,
            grid=(n_rt,),
            in_specs=[pl.BlockSpec((MOE_TM, d), lambda i, te, nu: (i, 0)),
                      pl.BlockSpec(memory_space=pl.ANY), pl.BlockSpec(memory_space=pl.ANY)],
            out_specs=pl.BlockSpec((MOE_TM, d), lambda i, te, nu: (i, 0)),
            scratch_shapes=_swiglu_scratch(d, d_ff)),
        out_shape=jax.ShapeDtypeStruct((n_rt * MOE_TM, d), F32),
        compiler_params=_cparams(("arbitrary",)),
        name="moe_grouped_ffn",
    )(tile_expert, n_used, xs, w_gu_all, w_down_all)

    return pl.pallas_call(
        _combine_kernel,
        grid=(n_tiles,),
        in_specs=[pl.BlockSpec((1, TOP_K, TM), lambda i: (i, 0, 0), memory_space=pltpu.SMEM),
                  pl.BlockSpec((TM, d), tok), lay.mod_spec(5, d), pl.BlockSpec((TM, TOP_K), tok),
                  pl.BlockSpec(memory_space=pl.ANY)],
        out_specs=pl.BlockSpec((TM, d), tok),
        out_shape=jax.ShapeDtypeStruct((n_rows, d), F32),
        scratch_shapes=[pltpu.VMEM((TOP_K, TM, d), F32), pltpu.SemaphoreType.DMA(())],
        compiler_params=_cparams(("arbitrary",)),
        name="moe_combine",
    )(pos_t, x, mods, weight, y)


def _final_norm_kernel(x_ref, g_ref, o_ref):
    x = x_ref[...]
    ms = jnp.mean(x * x, axis=-1, keepdims=True)
    o_ref[...] = (x * lax.rsqrt(ms + EPS)) * g_ref[...]


def _final_norm(lay, x, g):
    d = x.shape[1]
    return pl.pallas_call(
        _final_norm_kernel,
        grid=(lay.lat_tiles,),
        in_specs=[pl.BlockSpec((TM, d), lambda i: (i, 0)), pl.BlockSpec((1, d), lambda i: (0, 0))],
        out_specs=pl.BlockSpec((TM, d), lambda i: (i, 0)),
        out_shape=jax.ShapeDtypeStruct((lay.n_lat, d), F32),
        compiler_params=_cparams(("parallel",)),
        name="final_norm",
    )(x, g.reshape(1, d))


def kernel(x, c, ctx, c_ctx, ada_w, ada_b, norm_g, final_g, lru_w_in, lru_conv_w, lru_conv_b, lru_w_gate, lru_b_gate, lru_lambda, lru_w_out, ret_w_in, ret_log_decay, ret_gn_gain, ret_w_out, swa_w_in, swa_sink, swa_w_out, ffn_w_gu, ffn_w_down, moe_router, moe_w_gu, moe_w_down):
    bsz, seq, d = x.shape
    n_ctx = ctx.shape[1]
    depth = ada_w.shape[0]
    lay = _Layout(bsz, seq, n_ctx)
    xs = jnp.concatenate([x.reshape(-1, d), ctx.reshape(-1, d)], axis=0)
    all_mods = _ada_mods(c, c_ctx, ada_w, ada_b)
    for i in range(depth):
        last = i == depth - 1
        mods = all_mods[i]
        j = i // N_MIXERS
        if i % N_MIXERS == 0:
            xs = _lru_mixer(lay, xs, norm_g[i, 0], mods, lru_w_in[j], lru_conv_w[j], lru_conv_b[j],
                            lru_w_gate[j], lru_b_gate[j], lru_lambda[j], lru_w_out[j])
        elif i % N_MIXERS == 1:
            xs = _ret_mixer(lay, xs, norm_g[i, 0], mods, ret_w_in[j], ret_log_decay[j], ret_gn_gain[j], ret_w_out[j])
        else:
            xs = _swa_mixer(lay, xs, norm_g[i, 0], mods, swa_w_in[j], swa_sink[j], swa_w_out[j])
        n_tiles = lay.lat_tiles if last else lay.n_tiles
        f = i // 2
        if i % 2 == 0:
            xs = _dense_ffn(lay, xs, norm_g[i, 1], mods, ffn_w_gu, ffn_w_down, f, n_tiles)
        else:
            xs = _moe_ffn(lay, xs, norm_g[i, 1], mods, moe_router[f], moe_w_gu, moe_w_down, f, n_tiles)
    return _final_norm(lay, xs, final_g).reshape(bsz, seq, d)
```
